```python
import math
import jax, jax.numpy as jnp
from jax import lax
import numpy as np

D_MODEL = 1024
BATCH = 2
SEQ = 8192
DEPTH = 1
DEC_BATCH = 128
DEC_SEQ = 1
PAST_LEN = 2048
PAGE_SIZE = 128

SSM_HEAD_DIM = 64
D_SSM = D_MODEL
SSM_HEADS = D_SSM // SSM_HEAD_DIM
SSM_GROUPS = 4
SSM_STATE = 128
CONV_WIDTH = 4
CONV_DIM = D_SSM + 2 * SSM_GROUPS * SSM_STATE
SSD_CHUNK = 128
HEAD_DIM = 64
ATT_HEADS = D_MODEL // HEAD_DIM
KV_HEADS = ATT_HEADS // 4
Q_PER_KV = ATT_HEADS // KV_HEADS
D_ATT = ATT_HEADS * HEAD_DIM
MOBA_BLOCK = 256
MOBA_TOPK = 3
MOBA_Q_BLOCK = 32
ATT_SCALE = HEAD_DIM ** -0.5
ROPE_THETA = 10000.0
NORM_EPS = 1e-5
DEEPNORM_ALPHA = (2 * DEPTH) ** 0.25
DEEPNORM_BETA = (8 * DEPTH) ** -0.25
IN_SPLITS = (D_SSM, D_SSM, SSM_GROUPS * SSM_STATE, SSM_GROUPS * SSM_STATE, SSM_HEADS,
             D_ATT, KV_HEADS * HEAD_DIM, KV_HEADS * HEAD_DIM, D_ATT, D_MODEL, D_MODEL)
D_IN_PROJ = sum(IN_SPLITS)
SPLIT_POINTS = tuple(int(p) for p in np.cumsum(IN_SPLITS)[:-1])

kernel_name = 'hybrid_ssd_moba_deepnorm_step'


def _layer_norm(h, g, b):
    hf = h.astype(jnp.float32)
    mu = jnp.mean(hf, axis=-1, keepdims=True)
    var = jnp.mean(jnp.square(hf - mu), axis=-1, keepdims=True)
    out = (hf - mu) * lax.rsqrt(var + NORM_EPS) * g.astype(jnp.float32) + b.astype(jnp.float32)
    return out.astype(h.dtype)


def _gated_rmsnorm(y, z, w):
    g = y * jax.nn.silu(z.astype(jnp.float32))
    gg = g.reshape(g.shape[:-1] + (SSM_GROUPS, D_SSM // SSM_GROUPS))
    gg = gg * lax.rsqrt(jnp.mean(jnp.square(gg), axis=-1, keepdims=True) + NORM_EPS)
    return gg.reshape(g.shape) * w.astype(jnp.float32)


def _rope(x, pos):
    half = x.shape[-1] // 2
    inv_freq = jnp.power(ROPE_THETA, -jnp.arange(half, dtype=jnp.float32) * 2.0 / x.shape[-1])
    ang = pos.astype(jnp.float32)[:, None] * inv_freq[None, :]
    cos = jnp.cos(ang)[None, :, None, :]
    sin = jnp.sin(ang)[None, :, None, :]
    xf = x.astype(jnp.float32)
    x1, x2 = xf[..., :half], xf[..., half:]
    return jnp.concatenate([x1 * cos - x2 * sin, x2 * cos + x1 * sin], axis=-1).astype(x.dtype)


def _causal_conv(xbc, prev, w, b):
    full = jnp.concatenate([prev.astype(xbc.dtype), xbc], axis=1)
    y = lax.conv_general_dilated(full, w[:, None, :].astype(xbc.dtype), window_strides=(1,), padding='VALID',
                                 dimension_numbers=('NWC', 'WIO', 'NWC'), feature_group_count=xbc.shape[-1])
    return jax.nn.silu(y + b), full[:, full.shape[1] - (CONV_WIDTH - 1):]


def _ssd_scan(x, dt, a, bm, cm, h0, chunk):
    f32 = jnp.float32
    n, l, nh, hp = x.shape
    g, s = bm.shape[2], bm.shape[3]
    r = nh // g
    nc = -(-l // chunk)
    pad = nc * chunk - l
    pw = ((0, 0), (0, pad))
    xdt = jnp.pad(x.astype(f32) * dt[..., None], pw + ((0, 0), (0, 0))).reshape(n, nc, chunk, g, r, hp)
    adt = jnp.pad(dt * a, pw + ((0, 0),)).reshape(n, nc, chunk, g, r)
    bc = jnp.pad(bm.astype(f32), pw + ((0, 0), (0, 0))).reshape(n, nc, chunk, g, s)
    cc = jnp.pad(cm.astype(f32), pw + ((0, 0), (0, 0))).reshape(n, nc, chunk, g, s)
    acs = jnp.cumsum(adt, axis=2)
    causal = jnp.tril(jnp.ones((chunk, chunk), dtype=bool))[None, None, :, :, None, None]
    seg = acs[:, :, :, None] - acs[:, :, None, :]
    decay = jnp.exp(jnp.where(causal, seg, -jnp.inf))
    cb = jnp.einsum('ncigs,ncjgs->ncijg', cc, bc)
    y_diag = jnp.einsum('ncijg,ncijgr,ncjgrp->ncigrp', cb, decay, xdt)
    to_end = jnp.exp(acs[:, :, -1:] - acs)
    states = jnp.einsum('ncjgs,ncjgr,ncjgrp->ncgrps', bc, to_end, xdt)
    chunk_decay = jnp.exp(acs[:, :, -1])

    def step(h, inp):
        st, cd = inp
        return cd[..., None, None] * h + st, h

    h_last, h_in = lax.scan(step, h0.astype(f32).reshape(n, g, r, hp, s),
                            (jnp.moveaxis(states, 1, 0), jnp.moveaxis(chunk_decay, 1, 0)))
    h_in = jnp.moveaxis(h_in, 0, 1)
    y_off = jnp.einsum('ncigs,ncgrps,ncigr->ncigrp', cc, h_in, jnp.exp(acs))
    y = (y_diag + y_off).reshape(n, nc * chunk, nh, hp)[:, :l]
    return y, h_last.reshape(n, nh, hp, s)


def _moba_chunk(q, qpos, kblk, vblk, kmean):
    n, nq = q.shape[0], q.shape[1]
    nb = kblk.shape[1]
    qblk = qpos[0] // MOBA_BLOCK
    qg = q.reshape(n, nq, KV_HEADS, Q_PER_KV, HEAD_DIM)
    s_blk = jnp.einsum('nqkgd,nbkd->nqkgb', qg.astype(jnp.float32), kmean).reshape(n, nq, ATT_HEADS, nb)
    s_blk = jnp.where(jnp.arange(nb) < qblk, s_blk, -jnp.inf)
    kk = min(MOBA_TOPK, nb)
    _, idx = lax.top_k(s_blk, kk)
    n_i = jnp.arange(n)[:, None, None, None]
    kv_i = (jnp.arange(ATT_HEADS) // Q_PER_KV)[None, None, :, None]
    ksel = kblk[n_i, idx, :, kv_i]
    vsel = vblk[n_i, idx, :, kv_i]
    s_sel = jnp.einsum('nqhd,nqhcjd->nqhcj', q, ksel).reshape(n, nq, ATT_HEADS, kk * MOBA_BLOCK)
    kown = lax.dynamic_index_in_dim(kblk, qblk, axis=1, keepdims=False)
    vown = lax.dynamic_index_in_dim(vblk, qblk, axis=1, keepdims=False)
    s_own = jnp.einsum('nqkgd,njkd->nqkgj', qg, kown).reshape(n, nq, ATT_HEADS, MOBA_BLOCK)
    sel_ok = jnp.broadcast_to(jnp.repeat(jnp.arange(kk) < qblk, MOBA_BLOCK)[None, :], (nq, kk * MOBA_BLOCK))
    own_ok = (qblk * MOBA_BLOCK + jnp.arange(MOBA_BLOCK))[None, :] <= qpos[:, None]
    mask = jnp.concatenate([sel_ok, own_ok], axis=-1)[None, :, None, :]
    s = jnp.concatenate([s_sel, s_own], axis=-1).astype(jnp.float32) * ATT_SCALE
    p = jax.nn.softmax(jnp.where(mask, s, -jnp.inf), axis=-1).astype(vblk.dtype)
    p_sel = p[..., :kk * MOBA_BLOCK].reshape(n, nq, ATT_HEADS, kk, MOBA_BLOCK)
    p_own = p[..., kk * MOBA_BLOCK:].reshape(n, nq, KV_HEADS, Q_PER_KV, MOBA_BLOCK)
    o_sel = jnp.einsum('nqhcj,nqhcjd->nqhd', p_sel, vsel)
    o_own = jnp.einsum('nqkgj,njkd->nqkgd', p_own, vown).reshape(n, nq, ATT_HEADS, HEAD_DIM)
    return o_sel + o_own


def _moba_attention(q, qpos, k, v, q_block):
    n, lk = k.shape[0], k.shape[1]
    nb = -(-lk // MOBA_BLOCK)
    pw = ((0, 0), (0, nb * MOBA_BLOCK - lk), (0, 0), (0, 0))
    kblk = jnp.pad(k, pw).reshape(n, nb, MOBA_BLOCK, KV_HEADS, HEAD_DIM)
    vblk = jnp.pad(v, pw).reshape(n, nb, MOBA_BLOCK, KV_HEADS, HEAD_DIM)
    kmean = jnp.mean(kblk.astype(jnp.float32), axis=2)
    lq = q.shape[1]
    nc = lq // q_block
    qs = jnp.swapaxes(q.reshape(n, nc, q_block, ATT_HEADS, HEAD_DIM), 0, 1)
    ps = qpos.reshape(nc, q_block)
    out = lax.map(lambda a: _moba_chunk(a[0], a[1], kblk, vblk, kmean), (qs, ps))
    return jnp.swapaxes(out, 0, 1).reshape(n, lq, ATT_HEADS, HEAD_DIM)


def _mixer_layer(x, pos, conv_prev, ssm_prev, k_past, v_past, q_block,
                 w_in, conv_w, conv_b, dt_bias, a_log, d_skip, ssm_norm_w, w_a_out, w_b_out, w_out, ln_g, ln_b):
    n, l, _ = x.shape
    u = x @ w_in
    z_a, x_a, b_a, c_a, dt_a, q, k, v, z_b, g_a, g_b = jnp.split(u, SPLIT_POINTS, axis=-1)
    xbc, conv_new = _causal_conv(jnp.concatenate([x_a, b_a, c_a], axis=-1), conv_prev, conv_w, conv_b)
    xs, bm, cm = jnp.split(xbc, (D_SSM, D_SSM + SSM_GROUPS * SSM_STATE), axis=-1)
    xs = xs.reshape(n, l, SSM_HEADS, SSM_HEAD_DIM)
    bm = bm.reshape(n, l, SSM_GROUPS, SSM_STATE)
    cm = cm.reshape(n, l, SSM_GROUPS, SSM_STATE)
    dt = jax.nn.softplus(dt_a.astype(jnp.float32) + dt_bias.astype(jnp.float32))
    a = -jnp.exp(a_log.astype(jnp.float32))
    ys, ssm_new = _ssd_scan(xs, dt, a, bm, cm, ssm_prev, min(SSD_CHUNK, l))
    ys = ys + d_skip.astype(jnp.float32)[:, None] * xs.astype(jnp.float32)
    ya = _gated_rmsnorm(ys.reshape(n, l, D_SSM), z_a, ssm_norm_w).astype(x.dtype)
    br_a = ya @ w_a_out
    q = _rope(q.reshape(n, l, ATT_HEADS, HEAD_DIM), pos)
    k = _rope(k.reshape(n, l, KV_HEADS, HEAD_DIM), pos)
    v = v.reshape(n, l, KV_HEADS, HEAD_DIM)
    if k_past is None:
        k_all, v_all = k, v
    else:
        k_all = jnp.concatenate([k_past.astype(k.dtype), k], axis=1)
        v_all = jnp.concatenate([v_past.astype(v.dtype), v], axis=1)
    o = _moba_attention(q, pos, k_all, v_all, q_block).reshape(n, l, D_ATT)
    br_b = (o * jax.nn.silu(z_b)) @ w_b_out
    mixed = jax.nn.sigmoid(g_a) * br_a + jax.nn.sigmoid(g_b) * br_b
    y = _layer_norm(DEEPNORM_ALPHA * x + mixed @ w_out, ln_g, ln_b)
    return y, k, v, conv_new, ssm_new.astype(x.dtype)


def setup_inputs(seed: int = 0) -> dict:
    key = jax.random.key(seed)
    ks = jax.random.split(key, 20)
    f32 = jnp.float32
    n_pages = PAST_LEN // PAGE_SIZE
    n_used = DEC_BATCH * n_pages
    n_pool = n_used + max(1, n_used // 4)

    def nrm(k, shape, scale):
        return scale * jax.random.normal(k, shape, f32)

    x_prompt = nrm(ks[0], (BATCH, SEQ, D_MODEL), 1.0)
    x_sample = nrm(ks[1], (DEC_BATCH, DEC_SEQ, D_MODEL), 1.0)
    cache_k = nrm(ks[2], (DEPTH, n_pool, PAGE_SIZE, KV_HEADS, HEAD_DIM), 1.0)
    cache_v = nrm(ks[3], (DEPTH, n_pool, PAGE_SIZE, KV_HEADS, HEAD_DIM), 1.0)
    state_conv = nrm(ks[4], (DEPTH, DEC_BATCH, CONV_WIDTH - 1, CONV_DIM), 1.0)
    state_ssm = nrm(ks[5], (DEPTH, DEC_BATCH, SSM_HEADS, SSM_HEAD_DIM, SSM_STATE), 0.5)
    page_table = jax.random.permutation(ks[6], n_pool)[:n_used].reshape(DEC_BATCH, n_pages).astype(jnp.int32)
    w_in = nrm(ks[7], (DEPTH, D_MODEL, D_IN_PROJ), D_MODEL ** -0.5)
    conv_w = nrm(ks[8], (DEPTH, CONV_WIDTH, CONV_DIM), CONV_WIDTH ** -0.5)
    conv_b = nrm(ks[9], (DEPTH, CONV_DIM), 0.02)
    dt0 = jnp.exp(jax.random.uniform(ks[10], (DEPTH, SSM_HEADS), f32, math.log(1e-3), math.log(1e-1)))
    dt_bias = dt0 + jnp.log(-jnp.expm1(-dt0))
    a_log = jnp.log(jax.random.uniform(ks[11], (DEPTH, SSM_HEADS), f32, 1.0, 16.0))
    d_skip = 1.0 + nrm(ks[12], (DEPTH, SSM_HEADS), 0.1)
    ssm_norm_w = 1.0 + nrm(ks[13], (DEPTH, D_SSM), 0.05)
    w_a_out = nrm(ks[14], (DEPTH, D_SSM, D_MODEL), DEEPNORM_BETA * D_SSM ** -0.5)
    w_b_out = nrm(ks[15], (DEPTH, D_ATT, D_MODEL), DEEPNORM_BETA * D_ATT ** -0.5)
    w_out = nrm(ks[16], (DEPTH, D_MODEL, D_MODEL), DEEPNORM_BETA * D_MODEL ** -0.5)
    ln_g = 1.0 + nrm(ks[17], (DEPTH, D_MODEL), 0.05)
    ln_b = nrm(ks[18], (DEPTH, D_MODEL), 0.02)
    return {'x_prompt': x_prompt, 'x_sample': x_sample, 'cache_k': cache_k, 'cache_v': cache_v,
            'state_conv': state_conv, 'state_ssm': state_ssm, 'page_table': page_table,
            'w_in': w_in, 'conv_w': conv_w, 'conv_b': conv_b, 'dt_bias': dt_bias, 'a_log': a_log,
            'd_skip': d_skip, 'ssm_norm_w': ssm_norm_w, 'w_a_out': w_a_out, 'w_b_out': w_b_out,
            'w_out': w_out, 'ln_g': ln_g, 'ln_b': ln_b}


def reference(x_prompt, x_sample, cache_k, cache_v, state_conv, state_ssm, page_table,
              w_in, conv_w, conv_b, dt_bias, a_log, d_skip, ssm_norm_w, w_a_out, w_b_out, w_out, ln_g, ln_b):
    n_p, l_p = x_prompt.shape[0], x_prompt.shape[1]
    n_s, l_s = x_sample.shape[0], x_sample.shape[1]
    past_len = page_table.shape[1] * cache_k.shape[2]
    pos_p = jnp.arange(l_p, dtype=jnp.int32)
    pos_s = past_len + jnp.arange(l_s, dtype=jnp.int32)
    xp, xs = x_prompt, x_sample
    kp, vp, ksm, vsm, cp, cs, sp, ss = [], [], [], [], [], [], [], []
    for i in range(DEPTH):
        params = (w_in[i], conv_w[i], conv_b[i], dt_bias[i], a_log[i], d_skip[i], ssm_norm_w[i],
                  w_a_out[i], w_b_out[i], w_out[i], ln_g[i], ln_b[i])
        conv0 = jnp.zeros((n_p, CONV_WIDTH - 1, CONV_DIM), xp.dtype)
        ssm0 = jnp.zeros((n_p, SSM_HEADS, SSM_HEAD_DIM, SSM_STATE), xp.dtype)
        xp, k_new, v_new, c_new, s_new = _mixer_layer(xp, pos_p, conv0, ssm0, None, None, MOBA_Q_BLOCK, *params)
        kp.append(k_new); vp.append(v_new); cp.append(c_new); sp.append(s_new)
        past_k = cache_k[i][page_table].reshape(n_s, past_len, KV_HEADS, HEAD_DIM)
        past_v = cache_v[i][page_table].reshape(n_s, past_len, KV_HEADS, HEAD_DIM)
        xs, k_new, v_new, c_new, s_new = _mixer_layer(xs, pos_s, state_conv[i], state_ssm[i], past_k, past_v, 1, *params)
        ksm.append(k_new); vsm.append(v_new); cs.append(c_new); ss.append(s_new)
    return (xp, xs, jnp.stack(kp), jnp.stack(vp), jnp.stack(ksm), jnp.stack(vsm),
            jnp.stack(cp), jnp.stack(cs), jnp.stack(sp), jnp.stack(ss))
```

```python
import functools
import math

import numpy as np
import jax
import jax.numpy as jnp
from jax import lax
from jax.experimental import pallas as pl
from jax.experimental.pallas import tpu as pltpu

F32 = jnp.float32
BF16 = jnp.bfloat16

D_MODEL = 1024
HEAD_DIM = 64
SSM_HEADS = 16
SSM_HEAD_DIM = 64
SSM_GROUPS = 4
SSM_STATE = 128
CONV_WIDTH = 4
D_SSM = SSM_HEADS * SSM_HEAD_DIM
CONV_DIM = D_SSM + 2 * SSM_GROUPS * SSM_STATE
SSD_CHUNK = 128
ATT_HEADS = 16
KV_HEADS = 4
Q_PER_KV = ATT_HEADS // KV_HEADS
D_ATT = ATT_HEADS * HEAD_DIM
D_KV = KV_HEADS * HEAD_DIM
MOBA_BLOCK = 256
MOBA_TOPK = 3
ATT_SCALE = HEAD_DIM ** -0.5
ROPE_THETA = 10000.0
NORM_EPS = 1e-5
DEPTH = 1
DEEPNORM_ALPHA = (2 * DEPTH) ** 0.25

LANES = 128
VMEM_LIMIT = 56 * 1024 * 1024

ROW_XBC = 0
ROW_ZA = CONV_DIM
ROW_Q = ROW_ZA + D_SSM
ROW_ZB = ROW_Q + D_ATT
ROW_GA = ROW_ZB + D_ATT
ROW_GB = ROW_GA + D_MODEL
ROW_V = ROW_GB + D_MODEL
ROW_DT = ROW_V + D_KV
UT_ROWS = 7680
UT_TN = 1536

NEG = -1e30
HI = lax.Precision.HIGHEST


def _silu(x):
    return x / (1.0 + jnp.exp(-x))


def _sigmoid(x):
    return 1.0 / (1.0 + jnp.exp(-x))


def _softplus(x):
    return jnp.maximum(x, 0.0) + jnp.log(1.0 + jnp.exp(-jnp.abs(x)))


def _cparams(sem):
    return pltpu.CompilerParams(dimension_semantics=sem, vmem_limit_bytes=VMEM_LIMIT)


def _inproj_kernel(x_ref, wt_ref, wkv_ref, ut_ref, k_ref, v_ref):
    x = x_ref[...].astype(BF16)
    ut_ref[...] = lax.dot_general(wt_ref[...], x, (((1,), (1,)), ((), ())),
                                  preferred_element_type=F32)

    @pl.when(pl.program_id(1) == 0)
    def _():
        kv = jnp.dot(x, wkv_ref[...], preferred_element_type=F32)
        k_ref[...] = kv[:, :D_KV]
        v_ref[...] = kv[:, D_KV:]


def _in_proj(x2d, w_t, w_kv, tm):
    t = x2d.shape[0]
    return pl.pallas_call(
        _inproj_kernel,
        grid=(t // tm, UT_ROWS // UT_TN),
        in_specs=[pl.BlockSpec((tm, D_MODEL), lambda i, j: (i, 0)),
                  pl.BlockSpec((UT_TN, D_MODEL), lambda i, j: (j, 0)),
                  pl.BlockSpec((D_MODEL, 2 * D_KV), lambda i, j: (0, 0))],
        out_specs=[pl.BlockSpec((UT_TN, tm), lambda i, j: (j, i)),
                   pl.BlockSpec((tm, D_KV), lambda i, j: (i, 0)),
                   pl.BlockSpec((tm, D_KV), lambda i, j: (i, 0))],
        out_shape=[jax.ShapeDtypeStruct((UT_ROWS, t), F32),
                   jax.ShapeDtypeStruct((t, D_KV), F32),
                   jax.ShapeDtypeStruct((t, D_KV), F32)],
        compiler_params=_cparams(("arbitrary", "arbitrary")),
        name="in_proj",
    )(x2d, w_t, w_kv)


def _rope_tok(k, cos, sin_signed):
    lane = lax.broadcasted_iota(jnp.int32, (k.shape[0], LANES), 1)
    first = (lane & (HEAD_DIM - 1)) < (HEAD_DIM // 2)
    outs = []
    for c in range(k.shape[1] // LANES):
        kk = k[:, c * LANES:(c + 1) * LANES]
        up = pltpu.roll(kk, LANES - HEAD_DIM // 2, 1)
        dn = pltpu.roll(kk, HEAD_DIM // 2, 1)
        outs.append(kk * cos + jnp.where(first, up, dn) * sin_signed)
    return jnp.concatenate(outs, axis=1)


def _ropek_kernel(k_ref, cos_ref, sin_ref, ko_ref, kb_ref, km_ref):
    kr = _rope_tok(k_ref[...], cos_ref[...], sin_ref[...])
    ko_ref[...] = kr
    kb_ref[...] = kr.astype(BF16)
    km_ref[0] = jnp.mean(kr, axis=0, keepdims=True)


def _rope_k(k2d, cos_tok, sin_tok, rows):
    t = k2d.shape[0]
    nt = t // rows
    per_seq = cos_tok.shape[0] // rows
    return pl.pallas_call(
        _ropek_kernel,
        grid=(nt,),
        in_specs=[pl.BlockSpec((rows, D_KV), lambda i: (i, 0)),
                  pl.BlockSpec((rows, LANES), lambda i: (i % per_seq, 0)),
                  pl.BlockSpec((rows, LANES), lambda i: (i % per_seq, 0))],
        out_specs=[pl.BlockSpec((rows, D_KV), lambda i: (i, 0)),
                   pl.BlockSpec((rows, D_KV), lambda i: (i, 0)),
                   pl.BlockSpec((1, 1, D_KV), lambda i: (i, 0, 0))],
        out_shape=[jax.ShapeDtypeStruct((t, D_KV), F32),
                   jax.ShapeDtypeStruct((t, D_KV), BF16),
                   jax.ShapeDtypeStruct((nt, 1, D_KV), F32)],
        compiler_params=_cparams(("arbitrary",)),
        name="rope_k",
    )(k2d, cos_tok, sin_tok)


def _gated_rmsnorm_t(y_t, z_t, w_b):
    g = y_t * _silu(z_t)
    gsz = D_SSM // SSM_GROUPS
    outs = []
    for i in range(SSM_GROUPS):
        gg = g[i * gsz:(i + 1) * gsz]
        ms = jnp.mean(gg * gg, axis=0, keepdims=True)
        outs.append(gg * lax.rsqrt(ms + NORM_EPS))
    return jnp.concatenate(outs, axis=0) * w_b


def _bcast_cols(col_ref, dst_ref):
    for w in range(col_ref.shape[1]):
        dst_ref[w] = jnp.broadcast_to(col_ref[:, w:w + 1], dst_ref.shape[1:])


def _ssd_prompt_kernel(xbc_ref, za_ref, dt_ref, cw_ref, dtb_ref, alog_ref, dsk_ref, nw_ref,
                       ya_ref, st_ref, carry_ref, cwb_ref, pb_ref):
    c = pl.program_id(1)
    L = SSD_CHUNK

    @pl.when((pl.program_id(0) == 0) & (c == 0))
    def _():
        _bcast_cols(cw_ref, cwb_ref)
        pb_ref[0] = jnp.broadcast_to(dsk_ref[...], pb_ref.shape[1:])
        pb_ref[1] = jnp.broadcast_to(nw_ref[...], pb_ref.shape[1:])

    @pl.when(c == 0)
    def _():
        carry_ref[...] = jnp.zeros_like(carry_ref)
        st_ref[...] = jnp.zeros_like(st_ref)

    cur = xbc_ref[...]
    prev = carry_ref[...]
    lane = lax.broadcasted_iota(jnp.int32, cur.shape, 1)
    acc = cur * cwb_ref[CONV_WIDTH - 1] + cwb_ref[CONV_WIDTH]
    for s in range(1, CONV_WIDTH):
        sh = jnp.where(lane >= s, pltpu.roll(cur, s, 1), pltpu.roll(prev, s, 1))
        acc = acc + sh * cwb_ref[CONV_WIDTH - 1 - s]
    carry_ref[...] = cur
    xbc = _silu(acc)
    xs_t = xbc[:D_SSM]
    bm_t = xbc[D_SSM:D_SSM + SSM_GROUPS * SSM_STATE]
    cm_t = xbc[D_SSM + SSM_GROUPS * SSM_STATE:]

    dt_t = _softplus(dt_ref[...] + dtb_ref[...])
    a_col = -jnp.exp(alog_ref[...])
    adt_t = dt_t * a_col
    ii = lax.broadcasted_iota(jnp.int32, (L, L), 0)
    jj = lax.broadcasted_iota(jnp.int32, (L, L), 1)
    upper = (ii <= jj).astype(F32)
    lower = (jj <= ii).astype(F32)
    acs_t = jnp.dot(adt_t, upper, precision=HI, preferred_element_type=F32)
    acs = lax.dot_general(lower, adt_t, (((1,), (1,)), ((), ())), precision=HI,
                          preferred_element_type=F32)
    causal_t = ii <= jj

    dsk_b = pb_ref[0]
    ys = []
    R = SSM_HEADS // SSM_GROUPS
    P = SSM_HEAD_DIM
    for g in range(SSM_GROUPS):
        bmg_t = bm_t[g * SSM_STATE:(g + 1) * SSM_STATE]
        cmg_t = cm_t[g * SSM_STATE:(g + 1) * SSM_STATE]
        bmg = bmg_t.T.astype(BF16)
        cmg_tb = cmg_t.astype(BF16)
        cb_t = jnp.dot(bmg, cmg_tb, preferred_element_type=F32)
        for r in range(R):
            h = g * R + r
            row = acs_t[h:h + 1, :]
            col = acs[:, h:h + 1]
            dec = jnp.exp(jnp.where(causal_t, row - col, NEG))
            m_t = (cb_t * dec).astype(BF16)
            xs_h = xs_t[h * P:(h + 1) * P]
            xdt_h = xs_h * dt_t[h:h + 1, :]
            y_diag = jnp.dot(xdt_h.astype(BF16), m_t, preferred_element_type=F32)
            hprev = st_ref[0, h * P:(h + 1) * P, :]
            y_off = jnp.dot(hprev.astype(BF16), cmg_tb, preferred_element_type=F32) * jnp.exp(row)
            last = acs_t[h:h + 1, L - 1:L]
            te = jnp.exp(last - row)
            st = jnp.dot((xdt_h * te).astype(BF16), bmg, preferred_element_type=F32)
            st_ref[0, h * P:(h + 1) * P, :] = jnp.exp(last) * hprev + st
            ys.append(y_diag + y_off + dsk_b[h * P:(h + 1) * P] * xs_h)
    y_t = jnp.concatenate(ys, axis=0)
    ya_ref[...] = _gated_rmsnorm_t(y_t, za_ref[...], pb_ref[1]).astype(BF16)


def _ssd_prompt(u_t, n, l, cw_cols, dtb_col, alog_col, dsk_col, nw_col):
    nc = l // SSD_CHUNK
    col = lambda b, c: b * nc + c
    const = lambda b, c: (0, 0)
    return pl.pallas_call(
        _ssd_prompt_kernel,
        grid=(n, nc),
        in_specs=[pl.BlockSpec((CONV_DIM, SSD_CHUNK), lambda b, c: (ROW_XBC // CONV_DIM, col(b, c))),
                  pl.BlockSpec((D_SSM, SSD_CHUNK), lambda b, c: (ROW_ZA // D_SSM, col(b, c))),
                  pl.BlockSpec((LANES, SSD_CHUNK), lambda b, c: (ROW_DT // LANES, col(b, c))),
                  pl.BlockSpec((CONV_DIM, CONV_WIDTH + 1), const),
                  pl.BlockSpec((LANES, 1), const),
                  pl.BlockSpec((LANES, 1), const),
                  pl.BlockSpec((D_SSM, 1), const),
                  pl.BlockSpec((D_SSM, 1), const)],
        out_specs=[pl.BlockSpec((D_SSM, SSD_CHUNK), lambda b, c: (0, col(b, c))),
                   pl.BlockSpec((1, D_SSM, SSM_STATE), lambda b, c: (b, 0, 0))],
        out_shape=[jax.ShapeDtypeStruct((D_SSM, n * l), BF16),
                   jax.ShapeDtypeStruct((n, D_SSM, SSM_STATE), F32)],
        scratch_shapes=[pltpu.VMEM((CONV_DIM, SSD_CHUNK), F32),
                        pltpu.VMEM((CONV_WIDTH + 1, CONV_DIM, LANES), F32),
                        pltpu.VMEM((2, D_SSM, LANES), F32)],
        compiler_params=_cparams(("arbitrary", "arbitrary")),
        name="ssd_prompt",
    )(u_t, u_t, u_t, cw_cols, dtb_col, alog_col, dsk_col, nw_col)


def _rope_heads_t(q_ref, cos, sin, dst_ref):
    nq = q_ref.shape[1]
    half = HEAD_DIM // 2
    for h in range(ATT_HEADS):
        x1 = q_ref[h * HEAD_DIM:h * HEAD_DIM + half, :]
        x2 = q_ref[h * HEAD_DIM + half:(h + 1) * HEAD_DIM, :]
        g = h // Q_PER_KV
        dst_ref[g * HEAD_DIM:g * HEAD_DIM + half, h * nq:(h + 1) * nq] = (x1 * cos - x2 * sin) * ATT_SCALE
        dst_ref[g * HEAD_DIM + half:(g + 1) * HEAD_DIM, h * nq:(h + 1) * nq] = (x2 * cos + x1 * sin) * ATT_SCALE


def _top3_mask(gate, nvalid, axis):
    nb = gate.shape[axis]
    bi = lax.broadcasted_iota(jnp.int32, gate.shape, axis).astype(F32)
    valid = bi < nvalid
    gate = jnp.where(valid, gate, -jnp.inf)
    sel = jnp.zeros(gate.shape, F32)
    for _ in range(MOBA_TOPK):
        mx = jnp.max(gate, axis=axis, keepdims=True)
        idx = jnp.min(jnp.where(gate == mx, bi, float(nb)), axis=axis, keepdims=True)
        pick = bi == idx
        sel = jnp.where(pick, 1.0, sel)
        gate = jnp.where(pick, -jnp.inf, gate)
    return jnp.where((sel > 0.0) & valid, 0.0, NEG)


def _attn_prompt_kernel(q_ref, cos_ref, sin_ref, k_ref, vt_ref, km_ref, o_ref,
                        qf_ref, qb_ref, bias_ref, m_ref, l_ref, acc_ref):
    t = pl.program_id(1)
    nq = MOBA_BLOCK
    gw = Q_PER_KV * nq

    @pl.when((pl.program_id(0) == 0) & (t == 0))
    def _():
        qf_ref[...] = jnp.zeros_like(qf_ref)

    _rope_heads_t(q_ref, cos_ref[...], sin_ref[...], qf_ref)
    qf = qf_ref[...]
    qb_ref[...] = qf.astype(BF16)
    gate = jnp.dot(km_ref[...], qf, precision=HI, preferred_element_type=F32)
    bias_ref[...] = _top3_mask(gate, t.astype(F32), 0)

    def pv(vtb, pb, g, init, alpha):
        for r in range(Q_PER_KV):
            h = g * Q_PER_KV + r
            new = jnp.dot(vtb[g * HEAD_DIM:(g + 1) * HEAD_DIM, :], pb[:, r * nq:(r + 1) * nq],
                          preferred_element_type=F32)
            rows = slice(h * HEAD_DIM, (h + 1) * HEAD_DIM)
            if init:
                acc_ref[rows, :] = new
            else:
                acc_ref[rows, :] = acc_ref[rows, :] * alpha[:, r * nq:(r + 1) * nq] + new

    kb = k_ref[pl.ds(pl.multiple_of(t * nq, nq), nq), :]
    vtb = vt_ref[t]
    ji = lax.broadcasted_iota(jnp.int32, (nq, gw), 0)
    qi = lax.broadcasted_iota(jnp.int32, (nq, gw), 1) & (nq - 1)
    causal = ji <= qi
    for g in range(KV_HEADS):
        cols = slice(g * gw, (g + 1) * gw)
        s = jnp.dot(kb, qb_ref[:, cols], preferred_element_type=F32)
        s = jnp.where(causal, s, NEG)
        m = jnp.max(s, axis=0, keepdims=True)
        p = jnp.exp(s - m)
        m_ref[:, cols] = m
        l_ref[:, cols] = jnp.sum(p, axis=0, keepdims=True)
        pv(vtb, p.astype(BF16), g, True, None)

    def body(bk, carry):
        kb = k_ref[pl.ds(pl.multiple_of(bk * nq, nq), nq), :]
        vtb = vt_ref[bk]
        brow = bias_ref[pl.ds(bk, 1), :]
        for g in range(KV_HEADS):
            cols = slice(g * gw, (g + 1) * gw)
            s = jnp.dot(kb, qb_ref[:, cols], preferred_element_type=F32) + brow[:, cols]
            m_old = m_ref[:, cols]
            m_new = jnp.maximum(m_old, jnp.max(s, axis=0, keepdims=True))
            alpha = jnp.exp(m_old - m_new)
            p = jnp.exp(s - m_new)
            l_ref[:, cols] = alpha * l_ref[:, cols] + jnp.sum(p, axis=0, keepdims=True)
            m_ref[:, cols] = m_new
            pv(vtb, p.astype(BF16), g, False, alpha)
        return carry

    lax.fori_loop(0, t, body, 0)

    for h in range(ATT_HEADS):
        rows = slice(h * HEAD_DIM, (h + 1) * HEAD_DIM)
        o_ref[rows, :] = acc_ref[rows, :] / l_ref[:, h * nq:(h + 1) * nq]


def _attn_prompt(u_t, k_bf, vt3, kmean, cos_t, sin_t, n, l):
    nq = MOBA_BLOCK
    nb = l // nq
    col = lambda b, t: b * nb + t
    return pl.pallas_call(
        _attn_prompt_kernel,
        grid=(n, nb),
        in_specs=[pl.BlockSpec((D_ATT, nq), lambda b, t: (ROW_Q // D_ATT, col(b, t))),
                  pl.BlockSpec((HEAD_DIM // 2, nq), lambda b, t: (0, t)),
                  pl.BlockSpec((HEAD_DIM // 2, nq), lambda b, t: (0, t)),
                  pl.BlockSpec((l, D_KV), lambda b, t: (b, 0)),
                  pl.BlockSpec((nb, D_KV, nq), lambda b, t: (b, 0, 0)),
                  pl.BlockSpec((None, nb, D_KV), lambda b, t: (b, 0, 0))],
        out_specs=pl.BlockSpec((D_ATT, nq), lambda b, t: (0, col(b, t))),
        out_shape=jax.ShapeDtypeStruct((D_ATT, n * l), F32),
        scratch_shapes=[pltpu.VMEM((D_KV, ATT_HEADS * nq), F32),
                        pltpu.VMEM((D_KV, ATT_HEADS * nq), BF16),
                        pltpu.VMEM((nb, ATT_HEADS * nq), F32),
                        pltpu.VMEM((1, ATT_HEADS * nq), F32),
                        pltpu.VMEM((1, ATT_HEADS * nq), F32),
                        pltpu.VMEM((D_ATT, nq), F32)],
        compiler_params=_cparams(("arbitrary", "arbitrary")),
        name="attn_prompt",
    )(u_t, cos_t, sin_t, k_bf, vt3, kmean)


def _merge_kernel(ya_ref, o_ref, zb_ref, ga_ref, gb_ref, x_ref, wa_ref, wb_ref, wo_ref, lng_ref, lnb_ref,
                  y_ref):
    br_a = jnp.dot(wa_ref[...], ya_ref[...], preferred_element_type=F32)
    ob = (o_ref[...] * _silu(zb_ref[...])).astype(BF16)
    br_b = jnp.dot(wb_ref[...], ob, preferred_element_type=F32)
    mixed = _sigmoid(ga_ref[...]) * br_a + _sigmoid(gb_ref[...]) * br_b
    out_t = jnp.dot(wo_ref[...], mixed.astype(BF16), preferred_element_type=F32)
    hsum = DEEPNORM_ALPHA * x_ref[...] + out_t.T
    mu = jnp.mean(hsum, axis=-1, keepdims=True)
    d = hsum - mu
    var = jnp.mean(d * d, axis=-1, keepdims=True)
    y_ref[...] = d * lax.rsqrt(var + NORM_EPS) * lng_ref[...] + lnb_ref[...]


def _merge(ya_t, o_t, u_t, x2d, wa_t, wb_t, wo_t, ln_g, ln_b, tm):
    t = x2d.shape[0]
    const = lambda i: (0, 0)
    wspec = pl.BlockSpec((D_MODEL, D_MODEL), const)
    return pl.pallas_call(
        _merge_kernel,
        grid=(t // tm,),
        in_specs=[pl.BlockSpec((D_SSM, tm), lambda i: (0, i)),
                  pl.BlockSpec((D_ATT, tm), lambda i: (0, i)),
                  pl.BlockSpec((D_ATT, tm), lambda i: (ROW_ZB // D_ATT, i)),
                  pl.BlockSpec((D_MODEL, tm), lambda i: (ROW_GA // D_MODEL, i)),
                  pl.BlockSpec((D_MODEL, tm), lambda i: (ROW_GB // D_MODEL, i)),
                  pl.BlockSpec((tm, D_MODEL), lambda i: (i, 0)),
                  wspec, wspec, wspec,
                  pl.BlockSpec((1, D_MODEL), const),
                  pl.BlockSpec((1, D_MODEL), const)],
        out_specs=pl.BlockSpec((tm, D_MODEL), lambda i: (i, 0)),
        out_shape=jax.ShapeDtypeStruct((t, D_MODEL), F32),
        compiler_params=_cparams(("arbitrary",)),
        name="merge",
    )(ya_t, o_t, u_t, u_t, u_t, x2d, wa_t, wb_t, wo_t, ln_g, ln_b)


SEQ_BLK = 8


def _split3(x):
    hi = x.astype(BF16)
    r1 = x - hi.astype(F32)
    mid = r1.astype(BF16)
    lo = (r1 - mid.astype(F32)).astype(BF16)
    return hi, mid, lo


def _ssd_sample_kernel(xbc_ref, za_ref, dt_ref, cst_ref, cw_ref, dtb_ref, alog_ref, dsk_ref, nw_ref, st_ref,
                       sto_ref, ya_ref, newtok_ref,
                       dec_ref, xdt_ref, bm_ref, cmt_ref, xs_ref, y_ref):
    i = pl.program_id(0)
    ns = xbc_ref.shape[1]
    P = SSM_HEAD_DIM
    gs = D_SSM // SSM_GROUPS

    @pl.when(i == 0)
    def _():
        pre = xbc_ref[...]
        acc = pre * jnp.broadcast_to(cw_ref[:, CONV_WIDTH - 1:CONV_WIDTH], pre.shape)
        acc = acc + jnp.broadcast_to(cw_ref[:, CONV_WIDTH:CONV_WIDTH + 1], pre.shape)
        for w in range(CONV_WIDTH - 1):
            acc = acc + cst_ref[w] * jnp.broadcast_to(cw_ref[:, w:w + 1], pre.shape)
        xbc = _silu(acc)
        newtok_ref[...] = pre.T
        xs_t = xbc[:D_SSM]
        xs_ref[...] = xs_t
        nbc = SSM_GROUPS * SSM_STATE
        bm_ref[...] = xbc[D_SSM:D_SSM + nbc].T
        cmt_ref[...] = xbc[D_SSM + nbc:]
        dt_t = _softplus(dt_ref[...] + dtb_ref[...])
        a_col = -jnp.exp(alog_ref[...])
        dec_t = jnp.exp(dt_t * a_col)
        for h in range(SSM_HEADS):
            rows = slice(h * P, (h + 1) * P)
            dec_ref[rows, :] = jnp.broadcast_to(dec_t[h:h + 1, :], (P, ns))
            xdt_ref[rows, :] = xs_t[rows] * dt_t[h:h + 1, :]
        y_ref[...] = jnp.zeros_like(y_ref)

    kk = lax.broadcasted_iota(jnp.int32, (ns, SEQ_BLK * SSM_STATE), 0)
    cc = lax.broadcasted_iota(jnp.int32, (ns, SEQ_BLK * SSM_STATE), 1)
    sel = (kk == i * SEQ_BLK + (cc >> 7)).astype(BF16)

    def bsel(x):
        hi, mid, lo = _split3(x)
        d = lambda a: jnp.dot(a, sel, preferred_element_type=F32)
        return d(hi) + d(mid) + d(lo)

    dec_b = bsel(dec_ref[...])
    xdt_b = bsel(xdt_ref[...])
    lane = lax.broadcasted_iota(jnp.int32, (SSM_STATE, ns), 1)
    for r in range(SEQ_BLK):
        n = i * SEQ_BLK + r
        cs = slice(r * SSM_STATE, (r + 1) * SSM_STATE)
        brow = bm_ref[pl.ds(n, 1), :]
        onehot = lane == n
        for g in range(SSM_GROUPS):
            rows = slice(g * gs, (g + 1) * gs)
            bg = brow[:, g * SSM_STATE:(g + 1) * SSM_STATE]
            hn = dec_b[rows, cs] * st_ref[r, rows, :] + xdt_b[rows, cs] * bg
            sto_ref[r, rows, :] = hn
            cg = jnp.where(onehot, cmt_ref[g * SSM_STATE:(g + 1) * SSM_STATE, :], 0.0)
            y_ref[rows, :] += jnp.dot(hn.astype(BF16), cg.astype(BF16), preferred_element_type=F32)

    @pl.when(i == pl.num_programs(0) - 1)
    def _():
        y_t = y_ref[...] + jnp.broadcast_to(dsk_ref[...], y_ref.shape) * xs_ref[...]
        w_b = jnp.broadcast_to(nw_ref[...], y_ref.shape)
        ya_ref[...] = _gated_rmsnorm_t(y_t, za_ref[...], w_b).astype(BF16)


def _ssd_sample(u_t, cst_t, cw_cols, dtb_col, alog_col, dsk_col, nw_col, state):
    ns = u_t.shape[1]
    const2 = lambda i: (0, 0)
    return pl.pallas_call(
        _ssd_sample_kernel,
        grid=(ns // SEQ_BLK,),
        in_specs=[pl.BlockSpec((CONV_DIM, ns), lambda i: (ROW_XBC // CONV_DIM, 0)),
                  pl.BlockSpec((D_SSM, ns), lambda i: (ROW_ZA // D_SSM, 0)),
                  pl.BlockSpec((LANES, ns), lambda i: (ROW_DT // LANES, 0)),
                  pl.BlockSpec((CONV_WIDTH - 1, CONV_DIM, ns), lambda i: (0, 0, 0)),
                  pl.BlockSpec((CONV_DIM, CONV_WIDTH + 1), const2),
                  pl.BlockSpec((LANES, 1), const2),
                  pl.BlockSpec((LANES, 1), const2),
                  pl.BlockSpec((D_SSM, 1), const2),
                  pl.BlockSpec((D_SSM, 1), const2),
                  pl.BlockSpec((SEQ_BLK, D_SSM, SSM_STATE), lambda i: (i, 0, 0))],
        out_specs=[pl.BlockSpec((SEQ_BLK, D_SSM, SSM_STATE), lambda i: (i, 0, 0)),
                   pl.BlockSpec((D_SSM, ns), const2),
                   pl.BlockSpec((ns, CONV_DIM), const2)],
        out_shape=[jax.ShapeDtypeStruct(state.shape, F32),
                   jax.ShapeDtypeStruct((D_SSM, ns), BF16),
                   jax.ShapeDtypeStruct((ns, CONV_DIM), F32)],
        scratch_shapes=[pltpu.VMEM((D_SSM, ns), F32),
                        pltpu.VMEM((D_SSM, ns), F32),
                        pltpu.VMEM((ns, SSM_GROUPS * SSM_STATE), F32),
                        pltpu.VMEM((SSM_GROUPS * SSM_STATE, ns), F32),
                        pltpu.VMEM((D_SSM, ns), F32),
                        pltpu.VMEM((D_SSM, ns), F32)],
        compiler_params=_cparams(("arbitrary",)),
        name="ssd_sample",
    )(u_t, u_t, u_t, cst_t, cw_cols, dtb_col, alog_col, dsk_col, nw_col, state)


def _qprep_sample_kernel(q_ref, cos_ref, sin_ref, qt_ref, qf_ref):
    ns = q_ref.shape[1]
    qf_ref[...] = jnp.zeros_like(qf_ref)
    _rope_heads_t(q_ref, cos_ref[...], sin_ref[...], qf_ref)
    for h in range(ATT_HEADS):
        qt_ref[h] = qf_ref[:, h * ns:(h + 1) * ns].T


def _qprep_sample(u_t, cos_t, sin_t):
    ns = u_t.shape[1]
    return pl.pallas_call(
        _qprep_sample_kernel,
        grid=(1,),
        in_specs=[pl.BlockSpec((D_ATT, ns), lambda i: (ROW_Q // D_ATT, 0)),
                  pl.BlockSpec((HEAD_DIM // 2, ns), lambda i: (0, 0)),
                  pl.BlockSpec((HEAD_DIM // 2, ns), lambda i: (0, 0))],
        out_specs=pl.BlockSpec((ATT_HEADS, ns, D_KV), lambda i: (0, 0, 0)),
        out_shape=jax.ShapeDtypeStruct((ATT_HEADS, ns, D_KV), F32),
        scratch_shapes=[pltpu.VMEM((D_KV, ATT_HEADS * ns), F32)],
        compiler_params=_cparams(("arbitrary",)),
        name="qprep_sample",
    )(u_t, cos_t, sin_t)


def _attn_sample_kernel(pt_ref, q_ref, kn_ref, vn_ref, *refs, n_pages, page):
    k_refs = refs[:n_pages]
    v_refs = refs[n_pages:2 * n_pages]
    o_ref = refs[2 * n_pages]
    s_ref = refs[2 * n_pages + 1]
    km_ref = refs[2 * n_pages + 2]
    ppb = MOBA_BLOCK // page
    nb = n_pages // ppb
    qf = q_ref[0]
    qb = qf.astype(BF16)
    nt = (((1,), (1,)), ((), ()))
    for b in range(nb):
        ksum = None
        for j in range(b * ppb, (b + 1) * ppb):
            kp = k_refs[j][...]
            s_ref[:, j * page:(j + 1) * page] = lax.dot_general(qb, kp.astype(BF16), nt,
                                                                preferred_element_type=F32)
            ps = jnp.sum(kp, axis=0, keepdims=True)
            ksum = ps if ksum is None else ksum + ps
        km_ref[b:b + 1, :] = ksum * (1.0 / MOBA_BLOCK)
    gate = lax.dot_general(qf, km_ref[...], nt, precision=HI, preferred_element_type=F32)
    bias = _top3_mask(gate, float(nb), 1)
    kb_i = lax.broadcasted_iota(jnp.int32, (nb, n_pages * page), 0)
    kj_i = lax.broadcasted_iota(jnp.int32, (nb, n_pages * page), 1) >> int(math.log2(MOBA_BLOCK))
    expand = (kb_i == kj_i).astype(F32)
    sel_keys = jnp.dot(jnp.where(bias == 0.0, 1.0, 0.0), expand, preferred_element_type=F32)
    s = jnp.where(sel_keys > 0.5, s_ref[...], NEG)
    kn = kn_ref[0]
    s_own = jnp.sum(qf * kn, axis=1, keepdims=True)
    m = jnp.maximum(jnp.max(s, axis=1, keepdims=True), s_own)
    p = jnp.exp(s - m)
    p_own = jnp.exp(s_own - m)
    l = jnp.sum(p, axis=1, keepdims=True) + p_own
    pb = p.astype(BF16)
    acc = p_own * vn_ref[0]
    for j in range(n_pages):
        acc = acc + jnp.dot(pb[:, j * page:(j + 1) * page], v_refs[j][...].astype(BF16),
                            preferred_element_type=F32)
    hh = lax.broadcasted_iota(jnp.int32, acc.shape, 0) >> int(math.log2(Q_PER_KV))
    cg = lax.broadcasted_iota(jnp.int32, acc.shape, 1) >> int(math.log2(HEAD_DIM))
    o_ref[0] = jnp.where(hh == cg, acc / l, 0.0)


def _attn_sample(page_table_flat, q_rows, k_new, v_new, ck, cv, n_pages):
    ns = q_rows.shape[0]
    page = ck.shape[1]

    def pspec(j):
        return pl.BlockSpec((None, page, D_KV), lambda n, pt, j=j: (pt[n * n_pages + j], 0, 0))

    row3 = lambda n, pt: (n, 0, 0)
    grid_spec = pltpu.PrefetchScalarGridSpec(
        num_scalar_prefetch=1,
        grid=(ns,),
        in_specs=[pl.BlockSpec((1, ATT_HEADS, D_KV), row3),
                  pl.BlockSpec((1, 1, D_KV), row3),
                  pl.BlockSpec((1, 1, D_KV), row3)]
                 + [pspec(j) for j in range(n_pages)] + [pspec(j) for j in range(n_pages)],
        out_specs=pl.BlockSpec((1, ATT_HEADS, D_KV), row3),
        scratch_shapes=[pltpu.VMEM((ATT_HEADS, n_pages * page), F32),
                        pltpu.VMEM((n_pages * page // MOBA_BLOCK, D_KV), F32)],
    )
    return pl.pallas_call(
        functools.partial(_attn_sample_kernel, n_pages=n_pages, page=page),
        grid_spec=grid_spec,
        out_shape=jax.ShapeDtypeStruct((ns, ATT_HEADS, D_KV), F32),
        compiler_params=_cparams(("arbitrary",)),
        name="attn_sample",
    )(page_table_flat, q_rows, k_new, v_new, *([ck] * n_pages), *([cv] * n_pages))


def _ocompact_kernel(o_ref, ot_ref):
    for h in range(ATT_HEADS):
        g = h // Q_PER_KV
        ot_ref[h * HEAD_DIM:(h + 1) * HEAD_DIM, :] = o_ref[h].T[g * HEAD_DIM:(g + 1) * HEAD_DIM, :]


def _ocompact(o_h):
    ns = o_h.shape[1]
    return pl.pallas_call(
        _ocompact_kernel,
        grid=(1,),
        in_specs=[pl.BlockSpec((ATT_HEADS, ns, D_KV), lambda i: (0, 0, 0))],
        out_specs=pl.BlockSpec((D_ATT, ns), lambda i: (0, 0)),
        out_shape=jax.ShapeDtypeStruct((D_ATT, ns), F32),
        compiler_params=_cparams(("arbitrary",)),
        name="o_compact",
    )(o_h)


def _rope_tables(pos):
    half = HEAD_DIM // 2
    inv_freq = jnp.power(ROPE_THETA, -jnp.arange(half, dtype=F32) * 2.0 / HEAD_DIM)
    ang = pos.astype(F32)[:, None] * inv_freq[None, :]
    cos, sin = jnp.cos(ang), jnp.sin(ang)
    cos_tok = jnp.tile(cos, (1, LANES // half))
    sin_tok = jnp.tile(jnp.concatenate([-sin, sin], axis=1), (1, LANES // HEAD_DIM))
    return cos.T, sin.T, cos_tok, sin_tok


def _prep_weights(w_in, conv_w, conv_b, dt_bias, a_log, d_skip, ssm_norm_w, w_a_out, w_b_out, w_out):
    sp = np.cumsum([D_SSM, D_SSM, SSM_GROUPS * SSM_STATE, SSM_GROUPS * SSM_STATE, SSM_HEADS,
                    D_ATT, D_KV, D_KV, D_ATT, D_MODEL])
    z_a, x_a, b_a, c_a, dt, q, k, v, z_b, g_a, g_b = jnp.split(w_in, [int(s) for s in sp], axis=1)
    cols = jnp.concatenate([x_a, b_a, c_a, z_a, q, z_b, g_a, g_b, v, dt], axis=1)
    cols = jnp.pad(cols, ((0, 0), (0, UT_ROWS - cols.shape[1])))
    w_t = cols.T.astype(BF16)
    w_kv = jnp.concatenate([k, v], axis=1).astype(BF16)
    cw_cols = jnp.concatenate([conv_w, conv_b[None, :]], axis=0).T
    pad = LANES - SSM_HEADS
    dtb_col = jnp.pad(dt_bias, (0, pad))[:, None]
    alog_col = jnp.pad(a_log, (0, pad))[:, None]
    dsk_col = jnp.repeat(d_skip, SSM_HEAD_DIM)[:, None]
    nw_col = ssm_norm_w[:, None]
    return (w_t, w_kv, cw_cols, dtb_col, alog_col, dsk_col, nw_col,
            w_a_out.T.astype(BF16), w_b_out.T.astype(BF16), w_out.T.astype(BF16))


def kernel(x_prompt, x_sample, cache_k, cache_v, state_conv, state_ssm, page_table,
           w_in, conv_w, conv_b, dt_bias, a_log, d_skip, ssm_norm_w, w_a_out, w_b_out, w_out, ln_g, ln_b):
    assert w_in.shape[0] == DEPTH
    n_p, l_p, _ = x_prompt.shape
    n_s, l_s, _ = x_sample.shape
    n_pages, page = page_table.shape[1], cache_k.shape[2]
    past_len = n_pages * page
    assert l_s == 1 and l_p % MOBA_BLOCK == 0 and past_len % MOBA_BLOCK == 0 and MOBA_BLOCK % page == 0
    assert n_s % LANES == 0 and past_len // MOBA_BLOCK >= MOBA_TOPK

    (w_t, w_kv, cw_cols, dtb_col, alog_col, dsk_col, nw_col, wa_t, wb_t, wo_t) = _prep_weights(
        w_in[0], conv_w[0], conv_b[0], dt_bias[0], a_log[0], d_skip[0], ssm_norm_w[0],
        w_a_out[0], w_b_out[0], w_out[0])
    lng, lnb = ln_g[0][None, :], ln_b[0][None, :]

    xp = x_prompt.reshape(n_p * l_p, D_MODEL)
    cos_t, sin_t, cos_tok, sin_tok = _rope_tables(jnp.arange(l_p, dtype=jnp.int32))
    u_t, k_pre, v_p = _in_proj(xp, w_t, w_kv, 1024 if (n_p * l_p) % 1024 == 0 else MOBA_BLOCK)
    k_p, k_bf, kmean = _rope_k(k_pre, cos_tok, sin_tok, MOBA_BLOCK)
    ya_t, ssm_p = _ssd_prompt(u_t, n_p, l_p, cw_cols, dtb_col, alog_col, dsk_col, nw_col)
    nb = l_p // MOBA_BLOCK
    vt3 = jnp.transpose(u_t[ROW_V:ROW_V + D_KV].reshape(D_KV, n_p * nb, MOBA_BLOCK), (1, 0, 2)).astype(BF16)
    o_t = _attn_prompt(u_t, k_bf, vt3, kmean.reshape(n_p, nb, D_KV), cos_t, sin_t, n_p, l_p)
    y_p = _merge(ya_t, o_t, u_t, xp, wa_t, wb_t, wo_t, lng, lnb, MOBA_BLOCK)
    conv_p = jnp.transpose(u_t[:CONV_DIM].reshape(CONV_DIM, n_p, l_p)[:, :, l_p - (CONV_WIDTH - 1):], (1, 2, 0))

    xs = x_sample.reshape(n_s, D_MODEL)
    pos_s = jnp.full((n_s,), past_len, dtype=jnp.int32)
    cos_s, sin_s, cos_stok, sin_stok = _rope_tables(pos_s)
    us_t, ks_pre, v_s = _in_proj(xs, w_t, w_kv, n_s)
    k_s, _, _ = _rope_k(ks_pre, cos_stok, sin_stok, n_s)
    cst_t = jnp.transpose(state_conv[0], (1, 2, 0))
    st_in = state_ssm[0].reshape(n_s, D_SSM, SSM_STATE)
    ssm_s, yas_t, new_tok = _ssd_sample(us_t, cst_t, cw_cols, dtb_col, alog_col, dsk_col, nw_col, st_in)
    q_rows = jnp.transpose(_qprep_sample(us_t, cos_s, sin_s), (1, 0, 2))
    ck = cache_k[0].reshape(cache_k.shape[1], page, D_KV)
    cv = cache_v[0].reshape(cache_v.shape[1], page, D_KV)
    o_s = _attn_sample(page_table.reshape(-1), q_rows, k_s[:, None, :], v_s[:, None, :], ck, cv, n_pages)
    os_t = _ocompact(jnp.transpose(o_s, (1, 0, 2)))
    y_s = _merge(yas_t, os_t, us_t, xs, wa_t, wb_t, wo_t, lng, lnb, n_s)
    conv_s = jnp.concatenate([state_conv[0][:, 1:], new_tok[:, None, :]], axis=1)

    return (y_p.reshape(n_p, l_p, D_MODEL), y_s.reshape(n_s, l_s, D_MODEL),
            k_p.reshape(1, n_p, l_p, KV_HEADS, HEAD_DIM), v_p.reshape(1, n_p, l_p, KV_HEADS, HEAD_DIM),
            k_s.reshape(1, n_s, l_s, KV_HEADS, HEAD_DIM), v_s.reshape(1, n_s, l_s, KV_HEADS, HEAD_DIM),
            conv_p[None], conv_s[None],
            ssm_p.reshape(1, n_p, SSM_HEADS, SSM_HEAD_DIM, SSM_STATE),
            ssm_s.reshape(1, n_s, SSM_HEADS, SSM_HEAD_DIM, SSM_STATE))
```

```python
import functools
import math

import numpy as np
import jax
import jax.numpy as jnp
from jax import lax
from jax.experimental import pallas as pl
from jax.experimental.pallas import tpu as pltpu

F32 = jnp.float32
BF16 = jnp.bfloat16

D_MODEL = 1024
HEAD_DIM = 64
SSM_HEADS = 16
SSM_HEAD_DIM = 64
SSM_GROUPS = 4
SSM_STATE = 128
CONV_WIDTH = 4
D_SSM = SSM_HEADS * SSM_HEAD_DIM
CONV_DIM = D_SSM + 2 * SSM_GROUPS * SSM_STATE
SSD_CHUNK = 128
ATT_HEADS = 16
KV_HEADS = 4
Q_PER_KV = ATT_HEADS // KV_HEADS
D_ATT = ATT_HEADS * HEAD_DIM
D_KV = KV_HEADS * HEAD_DIM
MOBA_BLOCK = 256
MOBA_TOPK = 3
ATT_SCALE = HEAD_DIM ** -0.5
ROPE_THETA = 10000.0
NORM_EPS = 1e-5
DEPTH = 1
DEEPNORM_ALPHA = (2 * DEPTH) ** 0.25

LANES = 128
VMEM_LIMIT = 56 * 1024 * 1024

ROW_XBC = 0
ROW_Q = CONV_DIM
ROW_DT = ROW_Q + D_ATT
UT_ROWS = ROW_DT + LANES
ROW_ZA = 0
ROW_ZB = ROW_ZA + D_SSM
ROW_GA = ROW_ZB + D_ATT
ROW_GB = ROW_GA + D_MODEL
GT_ROWS = ROW_GB + D_MODEL
PROJ_STEPS = 4
UT_TN = UT_ROWS // PROJ_STEPS
GT_TN = GT_ROWS // PROJ_STEPS

NEG = -1e30
HI = lax.Precision.HIGHEST


def _silu(x):
    return x / (1.0 + jnp.exp(-x))


def _sigmoid(x):
    return 1.0 / (1.0 + jnp.exp(-x))


def _softplus(x):
    return jnp.maximum(x, 0.0) + jnp.log(1.0 + jnp.exp(-jnp.abs(x)))


def _cparams(sem):
    return pltpu.CompilerParams(dimension_semantics=sem, vmem_limit_bytes=VMEM_LIMIT)


def _inproj_kernel(x_ref, wt_ref, wkv_ref, ut_ref, gt_ref, k_ref, v_ref, vb_ref):
    x = x_ref[...].astype(BF16)
    nt = (((1,), (1,)), ((), ()))
    res = lax.dot_general(wt_ref[...], x, nt, preferred_element_type=F32)
    ut_ref[...] = res[:UT_TN]
    gt_ref[...] = res[UT_TN:].astype(BF16)

    @pl.when(pl.program_id(1) == 0)
    def _():
        kv = lax.dot_general(wkv_ref[...], x, nt, preferred_element_type=F32)
        k_ref[...] = kv[:D_KV]
        v_ref[...] = kv[D_KV:]
        wb = vb_ref.shape[2]
        for c in range(vb_ref.shape[0]):
            vb_ref[c] = kv[D_KV:, c * wb:(c + 1) * wb].astype(BF16)


def _in_proj(x2d, w_t, w_kv, tm, n, l):
    t = x2d.shape[0]
    per = l // tm
    wb = min(tm, MOBA_BLOCK)
    kv_spec = pl.BlockSpec((None, D_KV, tm), lambda i, j: (i // per, 0, i % per))
    return pl.pallas_call(
        _inproj_kernel,
        grid=(t // tm, PROJ_STEPS),
        in_specs=[pl.BlockSpec((tm, D_MODEL), lambda i, j: (i, 0)),
                  pl.BlockSpec((UT_TN + GT_TN, D_MODEL), lambda i, j: (j, 0)),
                  pl.BlockSpec((2 * D_KV, D_MODEL), lambda i, j: (0, 0))],
        out_specs=[pl.BlockSpec((UT_TN, tm), lambda i, j: (j, i)),
                   pl.BlockSpec((GT_TN, tm), lambda i, j: (j, i)), kv_spec, kv_spec,
                   pl.BlockSpec((tm // wb, D_KV, wb), lambda i, j: (i, 0, 0))],
        out_shape=[jax.ShapeDtypeStruct((UT_ROWS, t), F32),
                   jax.ShapeDtypeStruct((GT_ROWS, t), BF16),
                   jax.ShapeDtypeStruct((n, D_KV, l), F32),
                   jax.ShapeDtypeStruct((n, D_KV, l), F32),
                   jax.ShapeDtypeStruct((t // wb, D_KV, wb), BF16)],
        compiler_params=_cparams(("arbitrary", "arbitrary")),
        name="in_proj",
    )(x2d, w_t, w_kv)


def _ropek_kernel(k_ref, cos_ref, sin_ref, ko_ref, kt_ref, km_ref, kn_ref):
    half = HEAD_DIM // 2
    nblk = km_ref.shape[0]
    blk = k_ref.shape[1] // nblk
    grp = (lax.broadcasted_iota(jnp.int32, (D_KV, LANES), 0) >> int(math.log2(HEAD_DIM))
           == lax.broadcasted_iota(jnp.int32, (D_KV, LANES), 1)).astype(F32)
    for i in range(nblk):
        sl = slice(i * blk, (i + 1) * blk)
        cos, sin = cos_ref[:, sl], sin_ref[:, sl]
        rows = []
        for g in range(KV_HEADS):
            x1 = k_ref[g * HEAD_DIM:g * HEAD_DIM + half, sl]
            x2 = k_ref[g * HEAD_DIM + half:(g + 1) * HEAD_DIM, sl]
            rows += [x1 * cos - x2 * sin, x2 * cos + x1 * sin]
        kr = jnp.concatenate(rows, axis=0)
        ko_ref[:, sl] = kr
        kt = kr.T
        kq = kt.astype(kt_ref.dtype)
        kt_ref[sl, :] = kq
        km_ref[i] = jnp.mean(kt, axis=0, keepdims=True)
        kf = kq.astype(F32)
        nrm2 = jnp.dot(kf * kf, grp, preferred_element_type=F32)
        kn_ref[i] = jnp.sqrt(jnp.max(nrm2, axis=0, keepdims=True)) * (1.0 + 2.0 ** -7)


def _rope_k(k_t, cos_t, sin_t, tile, blk, tok_dtype):
    n, _, l = k_t.shape
    nt = l // tile
    per = tile // blk
    return pl.pallas_call(
        _ropek_kernel,
        grid=(n, nt),
        in_specs=[pl.BlockSpec((None, D_KV, tile), lambda b, i: (b, 0, i)),
                  pl.BlockSpec((HEAD_DIM // 2, tile), lambda b, i: (0, i)),
                  pl.BlockSpec((HEAD_DIM // 2, tile), lambda b, i: (0, i))],
        out_specs=[pl.BlockSpec((None, D_KV, tile), lambda b, i: (b, 0, i)),
                   pl.BlockSpec((tile, D_KV), lambda b, i: (b * nt + i, 0)),
                   pl.BlockSpec((per, 1, D_KV), lambda b, i: (b * nt + i, 0, 0)),
                   pl.BlockSpec((per, 1, LANES), lambda b, i: (b * nt + i, 0, 0))],
        out_shape=[jax.ShapeDtypeStruct((n, D_KV, l), F32),
                   jax.ShapeDtypeStruct((n * l, D_KV), tok_dtype),
                   jax.ShapeDtypeStruct((n * nt * per, 1, D_KV), F32),
                   jax.ShapeDtypeStruct((n * nt * per, 1, LANES), F32)],
        compiler_params=_cparams(("arbitrary", "arbitrary")),
        name="rope_k",
    )(k_t, cos_t, sin_t)


def _gated_rmsnorm_t(y_t, z_t, w_b):
    g = y_t * _silu(z_t)
    gsz = D_SSM // SSM_GROUPS
    outs = []
    for i in range(SSM_GROUPS):
        gg = g[i * gsz:(i + 1) * gsz]
        ms = jnp.mean(gg * gg, axis=0, keepdims=True)
        outs.append(gg * lax.rsqrt(ms + NORM_EPS))
    return jnp.concatenate(outs, axis=0) * w_b


def _bcast_cols(col_ref, dst_ref):
    for w in range(col_ref.shape[1]):
        dst_ref[w] = jnp.broadcast_to(col_ref[:, w:w + 1], dst_ref.shape[1:])


def _ssd_prompt_kernel(xbc_ref, za_ref, dt_ref, cw_ref, dtb_ref, alog_ref, dsk_ref, nw_ref,
                       ya_ref, st_ref, carry_ref, cwb_ref, pb_ref):
    c = pl.program_id(1)
    L = SSD_CHUNK

    @pl.when((pl.program_id(0) == 0) & (c == 0))
    def _():
        _bcast_cols(cw_ref, cwb_ref)
        pb_ref[0] = jnp.broadcast_to(dsk_ref[...], pb_ref.shape[1:])
        pb_ref[1] = jnp.broadcast_to(nw_ref[...], pb_ref.shape[1:])

    @pl.when(c == 0)
    def _():
        carry_ref[...] = jnp.zeros_like(carry_ref)
        st_ref[...] = jnp.zeros_like(st_ref)

    for k in range(xbc_ref.shape[1] // L):
        _ssd_chunk(xbc_ref, za_ref, dt_ref, dtb_ref, alog_ref, ya_ref, st_ref, carry_ref, cwb_ref, pb_ref,
                   slice(k * L, (k + 1) * L))


def _ssd_chunk(xbc_ref, za_ref, dt_ref, dtb_ref, alog_ref, ya_ref, st_ref, carry_ref, cwb_ref, pb_ref, sl):
    L = SSD_CHUNK
    cur = xbc_ref[:, sl]
    lane = lax.broadcasted_iota(jnp.int32, cur.shape, 1)
    acc = cur * cwb_ref[CONV_WIDTH - 1] + cwb_ref[CONV_WIDTH]
    for s in range(1, CONV_WIDTH):
        rolled = pltpu.roll(cur, s, 1)
        sh = jnp.where(lane >= s, rolled, carry_ref[s - 1])
        carry_ref[s - 1] = rolled
        acc = acc + sh * cwb_ref[CONV_WIDTH - 1 - s]
    xbc = _silu(acc)
    xs_t = xbc[:D_SSM]
    bm_t = xbc[D_SSM:D_SSM + SSM_GROUPS * SSM_STATE]
    cm_t = xbc[D_SSM + SSM_GROUPS * SSM_STATE:]

    dt_t = _softplus(dt_ref[:, sl] + dtb_ref[...])
    a_col = -jnp.exp(alog_ref[...])
    adt_t = dt_t * a_col
    ii = lax.broadcasted_iota(jnp.int32, (L, L), 0)
    jj = lax.broadcasted_iota(jnp.int32, (L, L), 1)
    upper = (ii <= jj).astype(F32)
    lower = (jj <= ii).astype(F32)
    acs_t = jnp.dot(adt_t, upper, precision=HI, preferred_element_type=F32)
    acs = lax.dot_general(lower, adt_t, (((1,), (1,)), ((), ())), precision=HI,
                          preferred_element_type=F32)
    causal_t = ii <= jj

    dsk_b = pb_ref[0]
    ys = []
    R = SSM_HEADS // SSM_GROUPS
    P = SSM_HEAD_DIM
    for g in range(SSM_GROUPS):
        bmg_t = bm_t[g * SSM_STATE:(g + 1) * SSM_STATE]
        cmg_t = cm_t[g * SSM_STATE:(g + 1) * SSM_STATE]
        bmg = bmg_t.T.astype(BF16)
        cmg_tb = cmg_t.astype(BF16)
        cb_t = jnp.dot(bmg, cmg_tb, preferred_element_type=F32)
        for r in range(R):
            h = g * R + r
            row = acs_t[h:h + 1, :]
            col = acs[:, h:h + 1]
            dec = jnp.exp(jnp.where(causal_t, row - col, NEG))
            m_t = (cb_t * dec).astype(BF16)
            xs_h = xs_t[h * P:(h + 1) * P]
            xdt_h = xs_h * dt_t[h:h + 1, :]
            y_diag = jnp.dot(xdt_h.astype(BF16), m_t, preferred_element_type=F32)
            hprev = st_ref[0, h * P:(h + 1) * P, :]
            y_off = jnp.dot(hprev.astype(BF16), cmg_tb, preferred_element_type=F32) * jnp.exp(row)
            last = acs_t[h:h + 1, L - 1:L]
            te = jnp.exp(last - row)
            st = jnp.dot((xdt_h * te).astype(BF16), bmg, preferred_element_type=F32)
            st_ref[0, h * P:(h + 1) * P, :] = jnp.exp(last) * hprev + st
            ys.append(y_diag + y_off + dsk_b[h * P:(h + 1) * P] * xs_h)
    y_t = jnp.concatenate(ys, axis=0)
    ya_ref[:, sl] = _gated_rmsnorm_t(y_t, za_ref[:, sl].astype(F32), pb_ref[1]).astype(BF16)


SSD_CHUNKS_PER_STEP = 2


def _ssd_prompt(u_t, g_t, n, l, cw_cols, dtb_col, alog_col, dsk_col, nw_col):
    tile = SSD_CHUNK * (SSD_CHUNKS_PER_STEP if l % (SSD_CHUNK * SSD_CHUNKS_PER_STEP) == 0 else 1)
    nc = l // tile
    col = lambda b, c: b * nc + c
    const = lambda b, c: (0, 0)
    return pl.pallas_call(
        _ssd_prompt_kernel,
        grid=(n, nc),
        in_specs=[pl.BlockSpec((CONV_DIM, tile), lambda b, c: (ROW_XBC // CONV_DIM, col(b, c))),
                  pl.BlockSpec((D_SSM, tile), lambda b, c: (ROW_ZA // D_SSM, col(b, c))),
                  pl.BlockSpec((LANES, tile), lambda b, c: (ROW_DT // LANES, col(b, c))),
                  pl.BlockSpec((CONV_DIM, CONV_WIDTH + 1), const),
                  pl.BlockSpec((LANES, 1), const),
                  pl.BlockSpec((LANES, 1), const),
                  pl.BlockSpec((D_SSM, 1), const),
                  pl.BlockSpec((D_SSM, 1), const)],
        out_specs=[pl.BlockSpec((D_SSM, tile), lambda b, c: (0, col(b, c))),
                   pl.BlockSpec((1, D_SSM, SSM_STATE), lambda b, c: (b, 0, 0))],
        out_shape=[jax.ShapeDtypeStruct((D_SSM, n * l), BF16),
                   jax.ShapeDtypeStruct((n, D_SSM, SSM_STATE), F32)],
        scratch_shapes=[pltpu.VMEM((CONV_WIDTH - 1, CONV_DIM, SSD_CHUNK), F32),
                        pltpu.VMEM((CONV_WIDTH + 1, CONV_DIM, LANES), F32),
                        pltpu.VMEM((2, D_SSM, LANES), F32)],
        compiler_params=_cparams(("arbitrary", "arbitrary")),
        name="ssd_prompt",
    )(u_t, g_t, u_t, cw_cols, dtb_col, alog_col, dsk_col, nw_col)


def _rope_heads_t(q_ref, cos, sin, dst_ref, scale):
    nq = q_ref.shape[1]
    half = HEAD_DIM // 2
    for h in range(ATT_HEADS):
        x1 = q_ref[h * HEAD_DIM:h * HEAD_DIM + half, :]
        x2 = q_ref[h * HEAD_DIM + half:(h + 1) * HEAD_DIM, :]
        g = h // Q_PER_KV
        dst_ref[g * HEAD_DIM:g * HEAD_DIM + half, h * nq:(h + 1) * nq] = (x1 * cos - x2 * sin) * scale
        dst_ref[g * HEAD_DIM + half:(g + 1) * HEAD_DIM, h * nq:(h + 1) * nq] = (x2 * cos + x1 * sin) * scale


def _top3_mask(gate, nvalid, axis):
    nb = gate.shape[axis]
    bi = lax.broadcasted_iota(jnp.int32, gate.shape, axis).astype(F32)
    valid = bi < nvalid
    gate = jnp.where(valid, gate, -jnp.inf)
    sel = jnp.zeros(gate.shape, F32)
    for _ in range(MOBA_TOPK):
        mx = jnp.max(gate, axis=axis, keepdims=True)
        idx = jnp.min(jnp.where(gate == mx, bi, float(nb)), axis=axis, keepdims=True)
        pick = bi == idx
        sel = jnp.where(pick, 1.0, sel)
        gate = jnp.where(pick, -jnp.inf, gate)
    return jnp.where((sel > 0.0) & valid, 0.0, NEG)


SUM_ROWS = 16
MAX_EXCESS = 60.0
SCORES_AHEAD = 4


def _attn_prompt_kernel(q_ref, cos_ref, sin_ref, k_ref, vt_ref, km_ref, kn_ref, o_ref,
                        qf_ref, qb_ref, bias_ref, m_ref, acc_ref):
    t = pl.program_id(1)
    nq = MOBA_BLOCK

    @pl.when((pl.program_id(0) == 0) & (t == 0))
    def _():
        qf_ref[...] = jnp.zeros_like(qf_ref)

    _rope_heads_t(q_ref, cos_ref[...], sin_ref[...], qf_ref, ATT_SCALE * math.log2(math.e))
    qb_ref[...] = qf_ref[...].astype(BF16)
    ones = jnp.ones((SUM_ROWS, nq), BF16)
    gw = Q_PER_KV * nq
    qn = []
    for g in range(KV_HEADS):
        qg = qf_ref[g * HEAD_DIM:(g + 1) * HEAD_DIM, g * gw:(g + 1) * gw]
        gate = jnp.dot(km_ref[g], qg, precision=HI, preferred_element_type=F32)
        bias_ref[:, g * gw:(g + 1) * gw] = _top3_mask(gate, t.astype(F32), 0)
        qn.append(jnp.sqrt(jnp.sum(qg * qg, axis=0, keepdims=True)) * (1.0 + 2.0 ** -7))
    causal = lax.broadcasted_iota(jnp.int32, (nq, nq), 0) <= lax.broadcasted_iota(jnp.int32, (nq, nq), 1)

    def run_blocks(blocks, fixed_max=False):
        loaded = []
        for bk, own in blocks:
            kb = k_ref[pl.ds(pl.multiple_of(bk * nq, nq), nq), :]
            brow = None if own else bias_ref[pl.ds(bk, 1), :]
            loaded.append((kb, vt_ref[bk], brow, own))
        items = [(j, h) for j in range(len(blocks)) for h in range(ATT_HEADS)]
        qk = lambda j, h: jnp.dot(loaded[j][0], qb_ref[:, h * nq:(h + 1) * nq], preferred_element_type=F32)
        pending = [qk(*it) for it in items[:SCORES_AHEAD]]
        for idx, (j, h) in enumerate(items):
            s = pending.pop(0)
            if idx + SCORES_AHEAD < len(items):
                pending.append(qk(*items[idx + SCORES_AHEAD]))
            _, vtb, brow, own = loaded[j]
            g = h // Q_PER_KV
            cols = slice(h * nq, (h + 1) * nq)
            if own:
                s = jnp.where(causal, s, NEG)
                m_new = jnp.max(s, axis=0, keepdims=True)
                sub = m_new
            elif fixed_max:
                m_new = None
                sub = jnp.where(brow[:, cols] == 0.0, m_ref[:, cols], -NEG)
            else:
                bh = brow[:, cols]
                m_old = m_ref[:, cols]
                m_new = jnp.maximum(m_old, jnp.max(s, axis=0, keepdims=True) + bh)
                sub = jnp.where(bh == 0.0, m_new, -NEG)
            p = jnp.exp2(s - sub).astype(BF16)
            va = jnp.concatenate([vtb[g * HEAD_DIM:(g + 1) * HEAD_DIM, :], ones], axis=0)
            pv = jnp.dot(va, p, preferred_element_type=F32)
            if own:
                acc_ref[h] = pv
                m_ref[:, cols] = m_new
            elif fixed_max:
                acc_ref[h] += pv
            else:
                acc_ref[h] = acc_ref[h] * jnp.exp2(m_old - m_new) + pv
                m_ref[:, cols] = m_new

    run_blocks([(t, True)])

    bi = lax.broadcasted_iota(jnp.int32, kn_ref.shape, 0)
    kmax = jnp.max(jnp.where(bi < t, kn_ref[...], 0.0), axis=0, keepdims=True)
    excess = [jnp.max(qn[g] * kmax[:, g:g + 1] - m_ref[:, g * gw:(g + 1) * gw]) for g in range(KV_HEADS)]
    fixed_ok = functools.reduce(jnp.maximum, excess) <= MAX_EXCESS
    odd = (t & 1) == 1

    @pl.when(fixed_ok)
    def _():
        def body(i, carry):
            run_blocks([(2 * i, False), (2 * i + 1, False)], fixed_max=True)
            return carry

        lax.fori_loop(0, t >> 1, body, 0)

        @pl.when(odd)
        def _():
            run_blocks([(t - 1, False)], fixed_max=True)

    @pl.when(jnp.logical_not(fixed_ok))
    def _():
        def body(i, carry):
            run_blocks([(2 * i, False), (2 * i + 1, False)])
            return carry

        lax.fori_loop(0, t >> 1, body, 0)

        @pl.when(odd)
        def _():
            run_blocks([(t - 1, False)])

    for h in range(ATT_HEADS):
        o_ref[h * HEAD_DIM:(h + 1) * HEAD_DIM, :] = (
            acc_ref[h, :HEAD_DIM, :] / acc_ref[h, HEAD_DIM:HEAD_DIM + 1, :]).astype(o_ref.dtype)


def _attn_prompt(u_t, k_bf, vt3, kmean, knmax, cos_t, sin_t, n, l):
    nq = MOBA_BLOCK
    nb = l // nq
    col = lambda b, t: b * nb + t
    return pl.pallas_call(
        _attn_prompt_kernel,
        grid=(n, nb),
        in_specs=[pl.BlockSpec((D_ATT, nq), lambda b, t: (ROW_Q // D_ATT, col(b, t))),
                  pl.BlockSpec((HEAD_DIM // 2, nq), lambda b, t: (0, t)),
                  pl.BlockSpec((HEAD_DIM // 2, nq), lambda b, t: (0, t)),
                  pl.BlockSpec((l, D_KV), lambda b, t: (b, 0)),
                  pl.BlockSpec((nb, D_KV, nq), lambda b, t: (b, 0, 0)),
                  pl.BlockSpec((None, KV_HEADS, nb, HEAD_DIM), lambda b, t: (b, 0, 0, 0)),
                  pl.BlockSpec((None, nb, LANES), lambda b, t: (b, 0, 0))],
        out_specs=pl.BlockSpec((D_ATT, nq), lambda b, t: (0, col(b, t))),
        out_shape=jax.ShapeDtypeStruct((D_ATT, n * l), BF16),
        scratch_shapes=[pltpu.VMEM((D_KV, ATT_HEADS * nq), F32),
                        pltpu.VMEM((D_KV, ATT_HEADS * nq), BF16),
                        pltpu.VMEM((nb, ATT_HEADS * nq), F32),
                        pltpu.VMEM((1, ATT_HEADS * nq), F32),
                        pltpu.VMEM((ATT_HEADS, HEAD_DIM + SUM_ROWS, nq), F32)],
        compiler_params=_cparams(("arbitrary", "arbitrary")),
        name="attn_prompt",
    )(u_t, cos_t, sin_t, k_bf, vt3, kmean, knmax)


def _merge_kernel(ya_ref, o_ref, zb_ref, ga_ref, gb_ref, x_ref, wa_ref, wb_ref, wo_ref, lng_ref, lnb_ref,
                  y_ref):
    br_a = jnp.dot(wa_ref[...], ya_ref[...], preferred_element_type=F32)
    ob = (o_ref[...].astype(F32) * _silu(zb_ref[...].astype(F32))).astype(BF16)
    br_b = jnp.dot(wb_ref[...], ob, preferred_element_type=F32)
    mixed = _sigmoid(ga_ref[...].astype(F32)) * br_a + _sigmoid(gb_ref[...].astype(F32)) * br_b
    out_t = jnp.dot(wo_ref[...], mixed.astype(BF16), preferred_element_type=F32)
    hsum = DEEPNORM_ALPHA * x_ref[...] + out_t.T
    mu = jnp.mean(hsum, axis=-1, keepdims=True)
    d = hsum - mu
    var = jnp.mean(d * d, axis=-1, keepdims=True)
    y_ref[...] = d * lax.rsqrt(var + NORM_EPS) * lng_ref[...] + lnb_ref[...]


def _merge(ya_t, o_t, g_t, x2d, wa_t, wb_t, wo_t, ln_g, ln_b, tm):
    t = x2d.shape[0]
    const = lambda i: (0, 0)
    wspec = pl.BlockSpec((D_MODEL, D_MODEL), const, pipeline_mode=pl.Buffered(1))
    return pl.pallas_call(
        _merge_kernel,
        grid=(t // tm,),
        in_specs=[pl.BlockSpec((D_SSM, tm), lambda i: (0, i)),
                  pl.BlockSpec((D_ATT, tm), lambda i: (0, i)),
                  pl.BlockSpec((D_ATT, tm), lambda i: (ROW_ZB // D_ATT, i)),
                  pl.BlockSpec((D_MODEL, tm), lambda i: (ROW_GA // D_MODEL, i)),
                  pl.BlockSpec((D_MODEL, tm), lambda i: (ROW_GB // D_MODEL, i)),
                  pl.BlockSpec((tm, D_MODEL), lambda i: (i, 0)),
                  wspec, wspec, wspec,
                  pl.BlockSpec((1, D_MODEL), const),
                  pl.BlockSpec((1, D_MODEL), const)],
        out_specs=pl.BlockSpec((tm, D_MODEL), lambda i: (i, 0)),
        out_shape=jax.ShapeDtypeStruct((t, D_MODEL), F32),
        compiler_params=_cparams(("arbitrary",)),
        name="merge",
    )(ya_t, o_t, g_t, g_t, g_t, x2d, wa_t, wb_t, wo_t, ln_g, ln_b)


SEQ_BLK = 8


def _split3(x):
    hi = x.astype(BF16)
    r1 = x - hi.astype(F32)
    mid = r1.astype(BF16)
    lo = (r1 - mid.astype(F32)).astype(BF16)
    return hi, mid, lo


def _ssd_sample_kernel(xbc_ref, za_ref, dt_ref, cst_ref, cw_ref, dtb_ref, alog_ref, dsk_ref, nw_ref, st_ref,
                       sto_ref, ya_ref, newtok_ref,
                       dec_ref, xdt_ref, bm_ref, cmt_ref, xs_ref, y_ref):
    i = pl.program_id(0)
    ns = xbc_ref.shape[1]
    P = SSM_HEAD_DIM
    gs = D_SSM // SSM_GROUPS

    @pl.when(i == 0)
    def _():
        pre = xbc_ref[...]
        acc = pre * jnp.broadcast_to(cw_ref[:, CONV_WIDTH - 1:CONV_WIDTH], pre.shape)
        acc = acc + jnp.broadcast_to(cw_ref[:, CONV_WIDTH:CONV_WIDTH + 1], pre.shape)
        for w in range(CONV_WIDTH - 1):
            acc = acc + cst_ref[w] * jnp.broadcast_to(cw_ref[:, w:w + 1], pre.shape)
        xbc = _silu(acc)
        newtok_ref[...] = pre.T
        xs_t = xbc[:D_SSM]
        xs_ref[...] = xs_t
        nbc = SSM_GROUPS * SSM_STATE
        bm_ref[...] = xbc[D_SSM:D_SSM + nbc].T
        cmt_ref[...] = xbc[D_SSM + nbc:]
        dt_t = _softplus(dt_ref[...] + dtb_ref[...])
        a_col = -jnp.exp(alog_ref[...])
        dec_t = jnp.exp(dt_t * a_col)
        for h in range(SSM_HEADS):
            rows = slice(h * P, (h + 1) * P)
            dec_ref[rows, :] = jnp.broadcast_to(dec_t[h:h + 1, :], (P, ns))
            xdt_ref[rows, :] = xs_t[rows] * dt_t[h:h + 1, :]
        y_ref[...] = jnp.zeros_like(y_ref)

    kk = lax.broadcasted_iota(jnp.int32, (ns, SEQ_BLK * SSM_STATE), 0)
    cc = lax.broadcasted_iota(jnp.int32, (ns, SEQ_BLK * SSM_STATE), 1)
    sel = (kk == i * SEQ_BLK + (cc >> 7)).astype(BF16)

    def bsel(x):
        hi, mid, lo = _split3(x)
        d = lambda a: jnp.dot(a, sel, preferred_element_type=F32)
        return d(hi) + d(mid) + d(lo)

    dec_b = bsel(dec_ref[...])
    xdt_b = bsel(xdt_ref[...])
    lane = lax.broadcasted_iota(jnp.int32, (SSM_STATE, ns), 1)
    for r in range(SEQ_BLK):
        n = i * SEQ_BLK + r
        cs = slice(r * SSM_STATE, (r + 1) * SSM_STATE)
        brow = bm_ref[pl.ds(n, 1), :]
        onehot = lane == n
        for g in range(SSM_GROUPS):
            rows = slice(g * gs, (g + 1) * gs)
            bg = brow[:, g * SSM_STATE:(g + 1) * SSM_STATE]
            hn = dec_b[rows, cs] * st_ref[r, rows, :] + xdt_b[rows, cs] * bg
            sto_ref[r, rows, :] = hn
            cg = jnp.where(onehot, cmt_ref[g * SSM_STATE:(g + 1) * SSM_STATE, :], 0.0)
            y_ref[rows, :] += jnp.dot(hn.astype(BF16), cg.astype(BF16), preferred_element_type=F32)

    @pl.when(i == pl.num_programs(0) - 1)
    def _():
        y_t = y_ref[...] + jnp.broadcast_to(dsk_ref[...], y_ref.shape) * xs_ref[...]
        w_b = jnp.broadcast_to(nw_ref[...], y_ref.shape)
        ya_ref[...] = _gated_rmsnorm_t(y_t, za_ref[...].astype(F32), w_b).astype(BF16)


def _ssd_sample(u_t, g_t, cst_t, cw_cols, dtb_col, alog_col, dsk_col, nw_col, state):
    ns = u_t.shape[1]
    const2 = lambda i: (0, 0)
    return pl.pallas_call(
        _ssd_sample_kernel,
        grid=(ns // SEQ_BLK,),
        in_specs=[pl.BlockSpec((CONV_DIM, ns), lambda i: (ROW_XBC // CONV_DIM, 0)),
                  pl.BlockSpec((D_SSM, ns), lambda i: (ROW_ZA // D_SSM, 0)),
                  pl.BlockSpec((LANES, ns), lambda i: (ROW_DT // LANES, 0)),
                  pl.BlockSpec((CONV_WIDTH - 1, CONV_DIM, ns), lambda i: (0, 0, 0)),
                  pl.BlockSpec((CONV_DIM, CONV_WIDTH + 1), const2),
                  pl.BlockSpec((LANES, 1), const2),
                  pl.BlockSpec((LANES, 1), const2),
                  pl.BlockSpec((D_SSM, 1), const2),
                  pl.BlockSpec((D_SSM, 1), const2),
                  pl.BlockSpec((SEQ_BLK, D_SSM, SSM_STATE), lambda i: (i, 0, 0))],
        out_specs=[pl.BlockSpec((SEQ_BLK, D_SSM, SSM_STATE), lambda i: (i, 0, 0)),
                   pl.BlockSpec((D_SSM, ns), const2),
                   pl.BlockSpec((ns, CONV_DIM), const2)],
        out_shape=[jax.ShapeDtypeStruct(state.shape, F32),
                   jax.ShapeDtypeStruct((D_SSM, ns), BF16),
                   jax.ShapeDtypeStruct((ns, CONV_DIM), F32)],
        scratch_shapes=[pltpu.VMEM((D_SSM, ns), F32),
                        pltpu.VMEM((D_SSM, ns), F32),
                        pltpu.VMEM((ns, SSM_GROUPS * SSM_STATE), F32),
                        pltpu.VMEM((SSM_GROUPS * SSM_STATE, ns), F32),
                        pltpu.VMEM((D_SSM, ns), F32),
                        pltpu.VMEM((D_SSM, ns), F32)],
        compiler_params=_cparams(("arbitrary",)),
        name="ssd_sample",
    )(u_t, g_t, u_t, cst_t, cw_cols, dtb_col, alog_col, dsk_col, nw_col, state)


def _qprep_sample_kernel(q_ref, cos_ref, sin_ref, qt_ref, qf_ref):
    ns = q_ref.shape[1]
    qf_ref[...] = jnp.zeros_like(qf_ref)
    _rope_heads_t(q_ref, cos_ref[...], sin_ref[...], qf_ref, ATT_SCALE)
    for h in range(ATT_HEADS):
        qt_ref[h] = qf_ref[:, h * ns:(h + 1) * ns].T


def _qprep_sample(u_t, cos_t, sin_t):
    ns = u_t.shape[1]
    return pl.pallas_call(
        _qprep_sample_kernel,
        grid=(1,),
        in_specs=[pl.BlockSpec((D_ATT, ns), lambda i: (ROW_Q // D_ATT, 0)),
                  pl.BlockSpec((HEAD_DIM // 2, ns), lambda i: (0, 0)),
                  pl.BlockSpec((HEAD_DIM // 2, ns), lambda i: (0, 0))],
        out_specs=pl.BlockSpec((ATT_HEADS, ns, D_KV), lambda i: (0, 0, 0)),
        out_shape=jax.ShapeDtypeStruct((ATT_HEADS, ns, D_KV), F32),
        scratch_shapes=[pltpu.VMEM((D_KV, ATT_HEADS * ns), F32)],
        compiler_params=_cparams(("arbitrary",)),
        name="qprep_sample",
    )(u_t, cos_t, sin_t)


def _attn_sample_kernel(pt_ref, q_ref, kn_ref, vn_ref, *refs, n_pages, page):
    sb = q_ref.shape[0]
    k_refs = refs[:sb * n_pages]
    v_refs = refs[sb * n_pages:2 * sb * n_pages]
    o_ref, s_ref = refs[2 * sb * n_pages:]
    nb = n_pages * page // MOBA_BLOCK
    nkeys = n_pages * page
    nh = ATT_HEADS

    s_own = []
    for i in range(sb):
        qf = q_ref[i]
        qb = qf.astype(BF16)
        for j in range(n_pages):
            kp = k_refs[i * n_pages + j][...]
            s_ref[i * nh:(i + 1) * nh, j * page:(j + 1) * page] = jnp.dot(
                qb, kp.astype(BF16), preferred_element_type=F32)
        s_own.append(jnp.sum(qf * kn_ref[i], axis=1, keepdims=True))
    s_own = jnp.concatenate(s_own, axis=0)
    lane = lax.broadcasted_iota(jnp.int32, (sb * nh, LANES), 1)
    gate = jnp.zeros((sb * nh, LANES), F32)
    for b in range(nb):
        bsum = jnp.sum(s_ref[:, b * MOBA_BLOCK:(b + 1) * MOBA_BLOCK], axis=1, keepdims=True)
        gate = jnp.where(lane == b, bsum, gate)
    bias = _top3_mask(gate, float(nb), 1)
    expand = (lax.broadcasted_iota(jnp.int32, (LANES, nkeys), 0)
              == (lax.broadcasted_iota(jnp.int32, (LANES, nkeys), 1) >> int(math.log2(MOBA_BLOCK)))).astype(F32)
    sel_keys = jnp.dot(jnp.where(bias == 0.0, 1.0, 0.0), expand, preferred_element_type=F32)
    s = jnp.where(sel_keys > 0.5, s_ref[...], NEG)
    m = jnp.maximum(jnp.max(s, axis=1, keepdims=True), s_own)
    p = jnp.exp(s - m)
    p_own = jnp.exp(s_own - m)
    l = jnp.sum(p, axis=1, keepdims=True) + p_own
    pb = p.astype(BF16)
    nt = (((1,), (1,)), ((), ()))
    hh = lax.broadcasted_iota(jnp.int32, (nh, D_KV), 0) >> int(math.log2(Q_PER_KV))
    cg = lax.broadcasted_iota(jnp.int32, (nh, D_KV), 1) >> int(math.log2(HEAD_DIM))
    for i in range(sb):
        rows = slice(i * nh, (i + 1) * nh)
        acc = p_own[rows] * vn_ref[i]
        for j in range(n_pages):
            acc = acc + lax.dot_general(pb[rows, j * page:(j + 1) * page],
                                        v_refs[i * n_pages + j][...].astype(BF16), nt,
                                        preferred_element_type=F32)
        o_ref[i] = jnp.where(hh == cg, acc / l[rows], 0.0)


ATT_SEQ_BLK = 4


def _attn_sample(page_table_flat, q_rows, k_new, v_new, ck_t, cv_t, n_pages):
    ns = q_rows.shape[0]
    page = ck_t.shape[2]
    sb = ATT_SEQ_BLK

    def pspec(i, j):
        return pl.BlockSpec((None, D_KV, page), lambda n, pt, i=i, j=j: (pt[(n * sb + i) * n_pages + j], 0, 0))

    pages = [pspec(i, j) for i in range(sb) for j in range(n_pages)]
    row3 = lambda n, pt: (n, 0, 0)
    grid_spec = pltpu.PrefetchScalarGridSpec(
        num_scalar_prefetch=1,
        grid=(ns // sb,),
        in_specs=[pl.BlockSpec((sb, ATT_HEADS, D_KV), row3),
                  pl.BlockSpec((sb, 1, D_KV), row3),
                  pl.BlockSpec((sb, 1, D_KV), row3)] + pages + pages,
        out_specs=pl.BlockSpec((sb, ATT_HEADS, D_KV), row3),
        scratch_shapes=[pltpu.VMEM((sb * ATT_HEADS, n_pages * page), F32)],
    )
    return pl.pallas_call(
        functools.partial(_attn_sample_kernel, n_pages=n_pages, page=page),
        grid_spec=grid_spec,
        out_shape=jax.ShapeDtypeStruct((ns, ATT_HEADS, D_KV), F32),
        compiler_params=_cparams(("arbitrary",)),
        name="attn_sample",
    )(page_table_flat, q_rows, k_new, v_new, *([ck_t] * (sb * n_pages)), *([cv_t] * (sb * n_pages)))


def _ocompact_kernel(o_ref, ot_ref):
    for h in range(ATT_HEADS):
        g = h // Q_PER_KV
        ot_ref[h * HEAD_DIM:(h + 1) * HEAD_DIM, :] = o_ref[h].T[g * HEAD_DIM:(g + 1) * HEAD_DIM, :]


def _ocompact(o_h):
    ns = o_h.shape[1]
    return pl.pallas_call(
        _ocompact_kernel,
        grid=(1,),
        in_specs=[pl.BlockSpec((ATT_HEADS, ns, D_KV), lambda i: (0, 0, 0))],
        out_specs=pl.BlockSpec((D_ATT, ns), lambda i: (0, 0)),
        out_shape=jax.ShapeDtypeStruct((D_ATT, ns), F32),
        compiler_params=_cparams(("arbitrary",)),
        name="o_compact",
    )(o_h)


def _rope_tables(pos):
    half = HEAD_DIM // 2
    inv_freq = jnp.power(ROPE_THETA, -jnp.arange(half, dtype=F32) * 2.0 / HEAD_DIM)
    ang = pos.astype(F32)[:, None] * inv_freq[None, :]
    return jnp.cos(ang).T, jnp.sin(ang).T


def _prep_weights(w_in, conv_w, conv_b, dt_bias, a_log, d_skip, ssm_norm_w, w_a_out, w_b_out, w_out):
    sp = np.cumsum([D_SSM, D_SSM, SSM_GROUPS * SSM_STATE, SSM_GROUPS * SSM_STATE, SSM_HEADS,
                    D_ATT, D_KV, D_KV, D_ATT, D_MODEL])
    z_a, x_a, b_a, c_a, dt, q, k, v, z_b, g_a, g_b = jnp.split(w_in, [int(s) for s in sp], axis=1)
    u_cols = jnp.concatenate([x_a, b_a, c_a, q, dt], axis=1)
    u_cols = jnp.pad(u_cols, ((0, 0), (0, UT_ROWS - u_cols.shape[1])))
    g_cols = jnp.concatenate([z_a, z_b, g_a, g_b], axis=1)
    w_t = jnp.concatenate([jnp.concatenate([u_cols[:, j * UT_TN:(j + 1) * UT_TN],
                                            g_cols[:, j * GT_TN:(j + 1) * GT_TN]], axis=1)
                           for j in range(PROJ_STEPS)], axis=1).T.astype(BF16)
    w_kv = jnp.concatenate([k, v], axis=1).T.astype(BF16)
    cw_cols = jnp.concatenate([conv_w, conv_b[None, :]], axis=0).T
    pad = LANES - SSM_HEADS
    dtb_col = jnp.pad(dt_bias, (0, pad))[:, None]
    alog_col = jnp.pad(a_log, (0, pad))[:, None]
    dsk_col = jnp.repeat(d_skip, SSM_HEAD_DIM)[:, None]
    nw_col = ssm_norm_w[:, None]
    return (w_t, w_kv, cw_cols, dtb_col, alog_col, dsk_col, nw_col,
            w_a_out.T.astype(BF16), w_b_out.T.astype(BF16), w_out.T.astype(BF16))


def kernel(x_prompt, x_sample, cache_k, cache_v, state_conv, state_ssm, page_table,
           w_in, conv_w, conv_b, dt_bias, a_log, d_skip, ssm_norm_w, w_a_out, w_b_out, w_out, ln_g, ln_b):
    assert w_in.shape[0] == DEPTH
    n_p, l_p, _ = x_prompt.shape
    n_s, l_s, _ = x_sample.shape
    n_pages, page = page_table.shape[1], cache_k.shape[2]
    past_len = n_pages * page
    assert l_s == 1 and l_p % MOBA_BLOCK == 0 and past_len % MOBA_BLOCK == 0 and MOBA_BLOCK % page == 0
    assert n_s % LANES == 0 and past_len // MOBA_BLOCK >= MOBA_TOPK

    (w_t, w_kv, cw_cols, dtb_col, alog_col, dsk_col, nw_col, wa_t, wb_t, wo_t) = _prep_weights(
        w_in[0], conv_w[0], conv_b[0], dt_bias[0], a_log[0], d_skip[0], ssm_norm_w[0],
        w_a_out[0], w_b_out[0], w_out[0])
    lng, lnb = ln_g[0][None, :], ln_b[0][None, :]

    xp = x_prompt.reshape(n_p * l_p, D_MODEL)
    cos_t, sin_t = _rope_tables(jnp.arange(l_p, dtype=jnp.int32))
    u_t, g_t, kp_t, vp_t, vt3 = _in_proj(xp, w_t, w_kv, 1024 if l_p % 1024 == 0 else MOBA_BLOCK, n_p, l_p)
    rope_tile = 4 * MOBA_BLOCK if l_p % (4 * MOBA_BLOCK) == 0 else MOBA_BLOCK
    kp_rt, k_bf, kmean, knmax = _rope_k(kp_t, cos_t, sin_t, rope_tile, MOBA_BLOCK, BF16)
    ya_t, ssm_p = _ssd_prompt(u_t, g_t, n_p, l_p, cw_cols, dtb_col, alog_col, dsk_col, nw_col)
    nb = l_p // MOBA_BLOCK
    kmean_g = jnp.transpose(kmean.reshape(n_p, nb, KV_HEADS, HEAD_DIM), (0, 2, 1, 3))
    o_t = _attn_prompt(u_t, k_bf, vt3, kmean_g, knmax.reshape(n_p, nb, LANES), cos_t, sin_t, n_p, l_p)
    to_cache = lambda a: jnp.transpose(a.reshape(a.shape[0], KV_HEADS, HEAD_DIM, a.shape[2]), (0, 3, 1, 2))[None]
    tm_merge = 2 * MOBA_BLOCK if (n_p * l_p) % (2 * MOBA_BLOCK) == 0 else MOBA_BLOCK
    y_p = _merge(ya_t, o_t, g_t, xp, wa_t, wb_t, wo_t, lng, lnb, tm_merge)
    conv_p = jnp.stack([u_t[:CONV_DIM, (b + 1) * l_p - (CONV_WIDTH - 1):(b + 1) * l_p].T for b in range(n_p)])

    xs = x_sample.reshape(n_s, D_MODEL)
    pos_s = jnp.full((n_s,), past_len, dtype=jnp.int32)
    cos_s, sin_s = _rope_tables(pos_s)
    us_t, gs_t, ks_t, vs_t, _ = _in_proj(xs, w_t, w_kv, n_s, 1, n_s)
    k_s = _rope_k(ks_t, cos_s, sin_s, n_s, n_s, F32)[1]
    v_s = vs_t[0].T
    cst_t = jnp.transpose(state_conv[0], (1, 2, 0))
    st_in = state_ssm[0].reshape(n_s, D_SSM, SSM_STATE)
    ssm_s, yas_t, new_tok = _ssd_sample(us_t, gs_t, cst_t, cw_cols, dtb_col, alog_col, dsk_col, nw_col, st_in)
    q_rows = jnp.transpose(_qprep_sample(us_t, cos_s, sin_s), (1, 0, 2))
    pages_t = lambda c: jnp.transpose(c[0], (0, 2, 3, 1)).reshape(c.shape[1], D_KV, page)
    o_s = _attn_sample(page_table.reshape(-1), q_rows, k_s[:, None, :], v_s[:, None, :],
                       pages_t(cache_k), pages_t(cache_v), n_pages)
    os_t = _ocompact(jnp.transpose(o_s, (1, 0, 2)))
    y_s = _merge(yas_t, os_t, gs_t, xs, wa_t, wb_t, wo_t, lng, lnb, n_s)
    conv_s = jnp.concatenate([state_conv[0][:, 1:], new_tok[:, None, :]], axis=1)

    return (y_p.reshape(n_p, l_p, D_MODEL), y_s.reshape(n_s, l_s, D_MODEL),
            to_cache(kp_rt), to_cache(vp_t),
            k_s.reshape(1, n_s, l_s, KV_HEADS, HEAD_DIM), v_s.reshape(1, n_s, l_s, KV_HEADS, HEAD_DIM),
            conv_p[None], conv_s[None],
            ssm_p.reshape(1, n_p, SSM_HEADS, SSM_HEAD_DIM, SSM_STATE),
            ssm_s.reshape(1, n_s, SSM_HEADS, SSM_HEAD_DIM, SSM_STATE))
```

```python
import functools
import math

import numpy as np
import jax
import jax.numpy as jnp
from jax import lax
from jax.experimental import pallas as pl
from jax.experimental.pallas import tpu as pltpu

F32 = jnp.float32
BF16 = jnp.bfloat16

D_MODEL = 1024
HEAD_DIM = 64
SSM_HEADS = 16
SSM_HEAD_DIM = 64
SSM_GROUPS = 4
SSM_STATE = 128
CONV_WIDTH = 4
D_SSM = SSM_HEADS * SSM_HEAD_DIM
CONV_DIM = D_SSM + 2 * SSM_GROUPS * SSM_STATE
SSD_CHUNK = 128
ATT_HEADS = 16
KV_HEADS = 4
Q_PER_KV = ATT_HEADS // KV_HEADS
D_ATT = ATT_HEADS * HEAD_DIM
D_KV = KV_HEADS * HEAD_DIM
MOBA_BLOCK = 256
MOBA_TOPK = 3
ATT_SCALE = HEAD_DIM ** -0.5
ROPE_THETA = 10000.0
NORM_EPS = 1e-5
DEPTH = 1
DEEPNORM_ALPHA = (2 * DEPTH) ** 0.25

LANES = 128
VMEM_LIMIT = 56 * 1024 * 1024

ROW_Q = 0
ROW_DT = ROW_Q + D_ATT
UT_ROWS = ROW_DT + LANES
ROW_ZA = 0
ROW_ZB = ROW_ZA + D_SSM
ROW_GA = ROW_ZB + D_ATT
ROW_GB = ROW_GA + D_MODEL
GT_ROWS = ROW_GB + D_MODEL
PROJ_STEPS = 4
UT_TN = UT_ROWS // PROJ_STEPS
GT_TN = GT_ROWS // PROJ_STEPS
XT_TN = CONV_DIM // PROJ_STEPS

NEG = -1e30
HI = lax.Precision.HIGHEST


def _silu(x):
    return x / (1.0 + jnp.exp(-x))


def _sigmoid(x):
    return 1.0 / (1.0 + jnp.exp(-x))


def _softplus(x):
    return jnp.maximum(x, 0.0) + jnp.log(1.0 + jnp.exp(-jnp.abs(x)))


def _cparams(sem):
    return pltpu.CompilerParams(dimension_semantics=sem, vmem_limit_bytes=VMEM_LIMIT)


def _inproj_kernel(x_ref, wt_ref, wx_ref, wkv_ref, ut_ref, gt_ref, xt_ref, k_ref, v_ref, vb_ref):
    x = x_ref[...].astype(BF16)
    nt = (((1,), (1,)), ((), ()))
    res = lax.dot_general(wt_ref[...], x, nt, preferred_element_type=F32)
    ut_ref[...] = res[:UT_TN]
    gt_ref[...] = res[UT_TN:].astype(BF16)
    xt_ref[...] = jnp.dot(x, wx_ref[...], preferred_element_type=F32)

    @pl.when(pl.program_id(1) == 0)
    def _():
        kv = lax.dot_general(wkv_ref[...], x, nt, preferred_element_type=F32)
        k_ref[...] = kv[:D_KV]
        v_ref[...] = kv[D_KV:]
        wb = vb_ref.shape[2]
        for c in range(vb_ref.shape[0]):
            vb_ref[c] = kv[D_KV:, c * wb:(c + 1) * wb].astype(BF16)


def _in_proj(x2d, w_t, w_x, w_kv, tm, n, l):
    t = x2d.shape[0]
    per = l // tm
    wb = min(tm, MOBA_BLOCK)
    kv_spec = pl.BlockSpec((None, D_KV, tm), lambda i, j: (i // per, 0, i % per))
    return pl.pallas_call(
        _inproj_kernel,
        grid=(t // tm, PROJ_STEPS),
        in_specs=[pl.BlockSpec((tm, D_MODEL), lambda i, j: (i, 0)),
                  pl.BlockSpec((UT_TN + GT_TN, D_MODEL), lambda i, j: (j, 0)),
                  pl.BlockSpec((D_MODEL, XT_TN), lambda i, j: (0, j)),
                  pl.BlockSpec((2 * D_KV, D_MODEL), lambda i, j: (0, 0))],
        out_specs=[pl.BlockSpec((UT_TN, tm), lambda i, j: (j, i)),
                   pl.BlockSpec((GT_TN, tm), lambda i, j: (j, i)),
                   pl.BlockSpec((tm, XT_TN), lambda i, j: (i, j)), kv_spec, kv_spec,
                   pl.BlockSpec((tm // wb, D_KV, wb), lambda i, j: (i, 0, 0))],
        out_shape=[jax.ShapeDtypeStruct((UT_ROWS, t), F32),
                   jax.ShapeDtypeStruct((GT_ROWS, t), BF16),
                   jax.ShapeDtypeStruct((t, CONV_DIM), F32),
                   jax.ShapeDtypeStruct((n, D_KV, l), F32),
                   jax.ShapeDtypeStruct((n, D_KV, l), F32),
                   jax.ShapeDtypeStruct((t // wb, D_KV, wb), BF16)],
        compiler_params=_cparams(("arbitrary", "arbitrary")),
        name="in_proj",
    )(x2d, w_t, w_x, w_kv)


def _ropek_kernel(k_ref, cos_ref, sin_ref, ko_ref, kt_ref, km_ref, kn_ref):
    half = HEAD_DIM // 2
    nblk = km_ref.shape[0]
    blk = k_ref.shape[1] // nblk
    grp = (lax.broadcasted_iota(jnp.int32, (D_KV, LANES), 0) >> int(math.log2(HEAD_DIM))
           == lax.broadcasted_iota(jnp.int32, (D_KV, LANES), 1)).astype(F32)
    for i in range(nblk):
        sl = slice(i * blk, (i + 1) * blk)
        cos, sin = cos_ref[:, sl], sin_ref[:, sl]
        rows = []
        for g in range(KV_HEADS):
            x1 = k_ref[g * HEAD_DIM:g * HEAD_DIM + half, sl]
            x2 = k_ref[g * HEAD_DIM + half:(g + 1) * HEAD_DIM, sl]
            rows += [x1 * cos - x2 * sin, x2 * cos + x1 * sin]
        kr = jnp.concatenate(rows, axis=0)
        ko_ref[:, sl] = kr
        kt = kr.T
        kq = kt.astype(kt_ref.dtype)
        kt_ref[sl, :] = kq
        km_ref[i] = jnp.mean(kt, axis=0, keepdims=True)
        kf = kq.astype(F32)
        nrm2 = jnp.dot(kf * kf, grp, preferred_element_type=F32)
        kn_ref[i] = jnp.sqrt(jnp.max(nrm2, axis=0, keepdims=True)) * (1.0 + 2.0 ** -7)


def _rope_k(k_t, cos_t, sin_t, tile, blk, tok_dtype):
    n, _, l = k_t.shape
    nt = l // tile
    per = tile // blk
    return pl.pallas_call(
        _ropek_kernel,
        grid=(n, nt),
        in_specs=[pl.BlockSpec((None, D_KV, tile), lambda b, i: (b, 0, i)),
                  pl.BlockSpec((HEAD_DIM // 2, tile), lambda b, i: (0, i)),
                  pl.BlockSpec((HEAD_DIM // 2, tile), lambda b, i: (0, i))],
        out_specs=[pl.BlockSpec((None, D_KV, tile), lambda b, i: (b, 0, i)),
                   pl.BlockSpec((tile, D_KV), lambda b, i: (b * nt + i, 0)),
                   pl.BlockSpec((per, 1, D_KV), lambda b, i: (b * nt + i, 0, 0)),
                   pl.BlockSpec((per, 1, LANES), lambda b, i: (b * nt + i, 0, 0))],
        out_shape=[jax.ShapeDtypeStruct((n, D_KV, l), F32),
                   jax.ShapeDtypeStruct((n * l, D_KV), tok_dtype),
                   jax.ShapeDtypeStruct((n * nt * per, 1, D_KV), F32),
                   jax.ShapeDtypeStruct((n * nt * per, 1, LANES), F32)],
        compiler_params=_cparams(("arbitrary", "arbitrary")),
        name="rope_k",
    )(k_t, cos_t, sin_t)


def _gated_rmsnorm_t(y_t, z_t, w_b):
    g = y_t * _silu(z_t)
    gsz = D_SSM // SSM_GROUPS
    outs = []
    for i in range(SSM_GROUPS):
        gg = g[i * gsz:(i + 1) * gsz]
        ms = jnp.mean(gg * gg, axis=0, keepdims=True)
        outs.append(gg * lax.rsqrt(ms + NORM_EPS))
    return jnp.concatenate(outs, axis=0) * w_b


CARRY_ROWS = 8


def _ssd_prompt_kernel(xbc_ref, za_ref, dt_ref, cw_ref, dtb_ref, alog_ref, dsk_ref, nw_ref,
                       ya_ref, st_ref, stage_ref, pb_ref):
    c = pl.program_id(1)
    L = SSD_CHUNK

    @pl.when((pl.program_id(0) == 0) & (c == 0))
    def _():
        pb_ref[0] = jnp.broadcast_to(dsk_ref[...], pb_ref.shape[1:])
        pb_ref[1] = jnp.broadcast_to(nw_ref[...], pb_ref.shape[1:])

    @pl.when(c == 0)
    def _():
        stage_ref[:CARRY_ROWS, :] = jnp.zeros((CARRY_ROWS, CONV_DIM), F32)
        st_ref[...] = jnp.zeros_like(st_ref)

    for k in range(xbc_ref.shape[0] // L):
        _ssd_chunk(xbc_ref, za_ref, dt_ref, cw_ref, dtb_ref, alog_ref, ya_ref, st_ref, stage_ref, pb_ref,
                   slice(k * L, (k + 1) * L))


def _ssd_chunk(xbc_ref, za_ref, dt_ref, cw_ref, dtb_ref, alog_ref, ya_ref, st_ref, stage_ref, pb_ref, sl):
    L = SSD_CHUNK
    cur = xbc_ref[sl, :]
    stage_ref[CARRY_ROWS:, :] = cur
    acc = cur * cw_ref[CONV_WIDTH - 1:CONV_WIDTH, :] + cw_ref[CONV_WIDTH:CONV_WIDTH + 1, :]
    for s in range(1, CONV_WIDTH):
        acc = acc + stage_ref[CARRY_ROWS - s:CARRY_ROWS - s + L, :] * cw_ref[CONV_WIDTH - 1 - s:CONV_WIDTH - s, :]
    stage_ref[:CARRY_ROWS, :] = cur[L - CARRY_ROWS:, :]
    xbc = _silu(acc)
    nbc = SSM_GROUPS * SSM_STATE
    xs_t = xbc[:, :D_SSM].T
    bm = xbc[:, D_SSM:D_SSM + nbc]
    cm_t = xbc[:, D_SSM + nbc:].T

    dt_t = _softplus(dt_ref[:, sl] + dtb_ref[...])
    a_col = -jnp.exp(alog_ref[...])
    adt_t = dt_t * a_col
    ii = lax.broadcasted_iota(jnp.int32, (L, L), 0)
    jj = lax.broadcasted_iota(jnp.int32, (L, L), 1)
    upper = (ii <= jj).astype(F32)
    lower = (jj <= ii).astype(F32)
    acs_t = jnp.dot(adt_t, upper, precision=HI, preferred_element_type=F32)
    acs = lax.dot_general(lower, adt_t, (((1,), (1,)), ((), ())), precision=HI,
                          preferred_element_type=F32)
    causal_t = ii <= jj

    dsk_b = pb_ref[0]
    ys = []
    R = SSM_HEADS // SSM_GROUPS
    P = SSM_HEAD_DIM
    for g in range(SSM_GROUPS):
        cmg_t = cm_t[g * SSM_STATE:(g + 1) * SSM_STATE]
        bmg = bm[:, g * SSM_STATE:(g + 1) * SSM_STATE].astype(BF16)
        cmg_tb = cmg_t.astype(BF16)
        cb_t = jnp.dot(bmg, cmg_tb, preferred_element_type=F32)
        for r in range(R):
            h = g * R + r
            row = acs_t[h:h + 1, :]
            col = acs[:, h:h + 1]
            dec = jnp.exp(jnp.where(causal_t, row - col, NEG))
            m_t = (cb_t * dec).astype(BF16)
            xs_h = xs_t[h * P:(h + 1) * P]
            xdt_h = xs_h * dt_t[h:h + 1, :]
            y_diag = jnp.dot(xdt_h.astype(BF16), m_t, preferred_element_type=F32)
            hprev = st_ref[0, h * P:(h + 1) * P, :]
            y_off = jnp.dot(hprev.astype(BF16), cmg_tb, preferred_element_type=F32) * jnp.exp(row)
            last = acs_t[h:h + 1, L - 1:L]
            te = jnp.exp(last - row)
            st = jnp.dot((xdt_h * te).astype(BF16), bmg, preferred_element_type=F32)
            st_ref[0, h * P:(h + 1) * P, :] = jnp.exp(last) * hprev + st
            ys.append(y_diag + y_off + dsk_b[h * P:(h + 1) * P] * xs_h)
    y_t = jnp.concatenate(ys, axis=0)
    ya_ref[:, sl] = _gated_rmsnorm_t(y_t, za_ref[:, sl].astype(F32), pb_ref[1]).astype(BF16)


SSD_CHUNKS_PER_STEP = 2


def _ssd_prompt(xbc_tok, u_t, g_t, n, l, cw_rows, dtb_col, alog_col, dsk_col, nw_col):
    tile = SSD_CHUNK * (SSD_CHUNKS_PER_STEP if l % (SSD_CHUNK * SSD_CHUNKS_PER_STEP) == 0 else 1)
    nc = l // tile
    col = lambda b, c: b * nc + c
    const = lambda b, c: (0, 0)
    return pl.pallas_call(
        _ssd_prompt_kernel,
        grid=(n, nc),
        in_specs=[pl.BlockSpec((tile, CONV_DIM), lambda b, c: (col(b, c), 0)),
                  pl.BlockSpec((D_SSM, tile), lambda b, c: (ROW_ZA // D_SSM, col(b, c))),
                  pl.BlockSpec((LANES, tile), lambda b, c: (ROW_DT // LANES, col(b, c))),
                  pl.BlockSpec((CARRY_ROWS, CONV_DIM), const),
                  pl.BlockSpec((LANES, 1), const),
                  pl.BlockSpec((LANES, 1), const),
                  pl.BlockSpec((D_SSM, 1), const),
                  pl.BlockSpec((D_SSM, 1), const)],
        out_specs=[pl.BlockSpec((D_SSM, tile), lambda b, c: (0, col(b, c))),
                   pl.BlockSpec((1, D_SSM, SSM_STATE), lambda b, c: (b, 0, 0))],
        out_shape=[jax.ShapeDtypeStruct((D_SSM, n * l), BF16),
                   jax.ShapeDtypeStruct((n, D_SSM, SSM_STATE), F32)],
        scratch_shapes=[pltpu.VMEM((CARRY_ROWS + SSD_CHUNK, CONV_DIM), F32),
                        pltpu.VMEM((2, D_SSM, LANES), F32)],
        compiler_params=_cparams(("arbitrary", "arbitrary")),
        name="ssd_prompt",
    )(xbc_tok, g_t, u_t, cw_rows, dtb_col, alog_col, dsk_col, nw_col)


def _rope_heads_t(q_ref, cos, sin, dst_ref, scale):
    nq = q_ref.shape[1]
    half = HEAD_DIM // 2
    for h in range(ATT_HEADS):
        x1 = q_ref[h * HEAD_DIM:h * HEAD_DIM + half, :]
        x2 = q_ref[h * HEAD_DIM + half:(h + 1) * HEAD_DIM, :]
        g = h // Q_PER_KV
        dst_ref[g * HEAD_DIM:g * HEAD_DIM + half, h * nq:(h + 1) * nq] = (x1 * cos - x2 * sin) * scale
        dst_ref[g * HEAD_DIM + half:(g + 1) * HEAD_DIM, h * nq:(h + 1) * nq] = (x2 * cos + x1 * sin) * scale


def _top3_mask(gate, nvalid, axis):
    nb = gate.shape[axis]
    bi = lax.broadcasted_iota(jnp.int32, gate.shape, axis).astype(F32)
    valid = bi < nvalid
    gate = jnp.where(valid, gate, -jnp.inf)
    sel = jnp.zeros(gate.shape, F32)
    for _ in range(MOBA_TOPK):
        mx = jnp.max(gate, axis=axis, keepdims=True)
        idx = jnp.min(jnp.where(gate == mx, bi, float(nb)), axis=axis, keepdims=True)
        pick = bi == idx
        sel = jnp.where(pick, 1.0, sel)
        gate = jnp.where(pick, -jnp.inf, gate)
    return jnp.where((sel > 0.0) & valid, 0.0, NEG)


SUM_ROWS = 16
MAX_EXCESS = 60.0
SCORES_AHEAD = 4


def _attn_prompt_kernel(q_ref, cos_ref, sin_ref, k_ref, vt_ref, km_ref, kn_ref, o_ref,
                        qf_ref, qb_ref, bias_ref, m_ref, acc_ref):
    t = pl.program_id(1)
    nq = MOBA_BLOCK

    @pl.when((pl.program_id(0) == 0) & (t == 0))
    def _():
        qf_ref[...] = jnp.zeros_like(qf_ref)

    _rope_heads_t(q_ref, cos_ref[...], sin_ref[...], qf_ref, ATT_SCALE * math.log2(math.e))
    qb_ref[...] = qf_ref[...].astype(BF16)
    ones = jnp.ones((SUM_ROWS, nq), BF16)
    gw = Q_PER_KV * nq
    qn = []
    for g in range(KV_HEADS):
        qg = qf_ref[g * HEAD_DIM:(g + 1) * HEAD_DIM, g * gw:(g + 1) * gw]
        gate = jnp.dot(km_ref[g], qg, precision=HI, preferred_element_type=F32)
        bias_ref[:, g * gw:(g + 1) * gw] = _top3_mask(gate, t.astype(F32), 0)
        qn.append(jnp.sqrt(jnp.sum(qg * qg, axis=0, keepdims=True)) * (1.0 + 2.0 ** -7))
    causal = lax.broadcasted_iota(jnp.int32, (nq, nq), 0) <= lax.broadcasted_iota(jnp.int32, (nq, nq), 1)

    def run_blocks(blocks, fixed_max=False):
        loaded = []
        for bk, own in blocks:
            kb = k_ref[pl.ds(pl.multiple_of(bk * nq, nq), nq), :]
            brow = None if own else bias_ref[pl.ds(bk, 1), :]
            loaded.append((kb, vt_ref[bk], brow, own))
        items = [(j, h) for j in range(len(blocks)) for h in range(ATT_HEADS)]
        qk = lambda j, h: jnp.dot(loaded[j][0], qb_ref[:, h * nq:(h + 1) * nq], preferred_element_type=F32)
        pending = [qk(*it) for it in items[:SCORES_AHEAD]]
        for idx, (j, h) in enumerate(items):
            s = pending.pop(0)
            if idx + SCORES_AHEAD < len(items):
                pending.append(qk(*items[idx + SCORES_AHEAD]))
            _, vtb, brow, own = loaded[j]
            g = h // Q_PER_KV
            cols = slice(h * nq, (h + 1) * nq)
            if own:
                s = jnp.where(causal, s, NEG)
                m_new = jnp.max(s, axis=0, keepdims=True)
                sub = m_new
            elif fixed_max:
                m_new = None
                sub = jnp.where(brow[:, cols] == 0.0, m_ref[:, cols], -NEG)
            else:
                bh = brow[:, cols]
                m_old = m_ref[:, cols]
                m_new = jnp.maximum(m_old, jnp.max(s, axis=0, keepdims=True) + bh)
                sub = jnp.where(bh == 0.0, m_new, -NEG)
            p = jnp.exp2(s - sub).astype(BF16)
            va = jnp.concatenate([vtb[g * HEAD_DIM:(g + 1) * HEAD_DIM, :], ones], axis=0)
            pv = jnp.dot(va, p, preferred_element_type=F32)
            if own:
                acc_ref[h] = pv
                m_ref[:, cols] = m_new
            elif fixed_max:
                acc_ref[h] += pv
            else:
                acc_ref[h] = acc_ref[h] * jnp.exp2(m_old - m_new) + pv
                m_ref[:, cols] = m_new

    run_blocks([(t, True)])

    bi = lax.broadcasted_iota(jnp.int32, kn_ref.shape, 0)
    kmax = jnp.max(jnp.where(bi < t, kn_ref[...], 0.0), axis=0, keepdims=True)
    excess = [jnp.max(qn[g] * kmax[:, g:g + 1] - m_ref[:, g * gw:(g + 1) * gw]) for g in range(KV_HEADS)]
    fixed_ok = functools.reduce(jnp.maximum, excess) <= MAX_EXCESS
    odd = (t & 1) == 1

    @pl.when(fixed_ok)
    def _():
        def body(i, carry):
            run_blocks([(2 * i, False), (2 * i + 1, False)], fixed_max=True)
            return carry

        lax.fori_loop(0, t >> 1, body, 0)

        @pl.when(odd)
        def _():
            run_blocks([(t - 1, False)], fixed_max=True)

    @pl.when(jnp.logical_not(fixed_ok))
    def _():
        def body(i, carry):
            run_blocks([(2 * i, False), (2 * i + 1, False)])
            return carry

        lax.fori_loop(0, t >> 1, body, 0)

        @pl.when(odd)
        def _():
            run_blocks([(t - 1, False)])

    for h in range(ATT_HEADS):
        o_ref[h * HEAD_DIM:(h + 1) * HEAD_DIM, :] = (
            acc_ref[h, :HEAD_DIM, :] / acc_ref[h, HEAD_DIM:HEAD_DIM + 1, :]).astype(o_ref.dtype)


def _attn_prompt(u_t, k_bf, vt3, kmean, knmax, cos_t, sin_t, n, l):
    nq = MOBA_BLOCK
    nb = l // nq
    col = lambda b, t: b * nb + t
    return pl.pallas_call(
        _attn_prompt_kernel,
        grid=(n, nb),
        in_specs=[pl.BlockSpec((D_ATT, nq), lambda b, t: (ROW_Q // D_ATT, col(b, t))),
                  pl.BlockSpec((HEAD_DIM // 2, nq), lambda b, t: (0, t)),
                  pl.BlockSpec((HEAD_DIM // 2, nq), lambda b, t: (0, t)),
                  pl.BlockSpec((l, D_KV), lambda b, t: (b, 0)),
                  pl.BlockSpec((nb, D_KV, nq), lambda b, t: (b, 0, 0)),
                  pl.BlockSpec((None, KV_HEADS, nb, HEAD_DIM), lambda b, t: (b, 0, 0, 0)),
                  pl.BlockSpec((None, nb, LANES), lambda b, t: (b, 0, 0))],
        out_specs=pl.BlockSpec((D_ATT, nq), lambda b, t: (0, col(b, t))),
        out_shape=jax.ShapeDtypeStruct((D_ATT, n * l), BF16),
        scratch_shapes=[pltpu.VMEM((D_KV, ATT_HEADS * nq), F32),
                        pltpu.VMEM((D_KV, ATT_HEADS * nq), BF16),
                        pltpu.VMEM((nb, ATT_HEADS * nq), F32),
                        pltpu.VMEM((1, ATT_HEADS * nq), F32),
                        pltpu.VMEM((ATT_HEADS, HEAD_DIM + SUM_ROWS, nq), F32)],
        compiler_params=_cparams(("arbitrary", "arbitrary")),
        name="attn_prompt",
    )(u_t, cos_t, sin_t, k_bf, vt3, kmean, knmax)


def _merge_kernel(ya_ref, o_ref, zb_ref, ga_ref, gb_ref, x_ref, wa_ref, wb_ref, wo_ref, lng_ref, lnb_ref,
                  y_ref):
    br_a = jnp.dot(wa_ref[...], ya_ref[...], preferred_element_type=F32)
    ob = (o_ref[...].astype(F32) * _silu(zb_ref[...].astype(F32))).astype(BF16)
    br_b = jnp.dot(wb_ref[...], ob, preferred_element_type=F32)
    mixed = _sigmoid(ga_ref[...].astype(F32)) * br_a + _sigmoid(gb_ref[...].astype(F32)) * br_b
    out_t = jnp.dot(wo_ref[...], mixed.astype(BF16), preferred_element_type=F32)
    hsum = DEEPNORM_ALPHA * x_ref[...] + out_t.T
    mu = jnp.mean(hsum, axis=-1, keepdims=True)
    d = hsum - mu
    var = jnp.mean(d * d, axis=-1, keepdims=True)
    y_ref[...] = d * lax.rsqrt(var + NORM_EPS) * lng_ref[...] + lnb_ref[...]


def _merge(ya_t, o_t, g_t, x2d, wa_t, wb_t, wo_t, ln_g, ln_b, tm):
    t = x2d.shape[0]
    const = lambda i: (0, 0)
    wspec = pl.BlockSpec((D_MODEL, D_MODEL), const, pipeline_mode=pl.Buffered(1))
    return pl.pallas_call(
        _merge_kernel,
        grid=(t // tm,),
        in_specs=[pl.BlockSpec((D_SSM, tm), lambda i: (0, i)),
                  pl.BlockSpec((D_ATT, tm), lambda i: (0, i)),
                  pl.BlockSpec((D_ATT, tm), lambda i: (ROW_ZB // D_ATT, i)),
                  pl.BlockSpec((D_MODEL, tm), lambda i: (ROW_GA // D_MODEL, i)),
                  pl.BlockSpec((D_MODEL, tm), lambda i: (ROW_GB // D_MODEL, i)),
                  pl.BlockSpec((tm, D_MODEL), lambda i: (i, 0)),
                  wspec, wspec, wspec,
                  pl.BlockSpec((1, D_MODEL), const),
                  pl.BlockSpec((1, D_MODEL), const)],
        out_specs=pl.BlockSpec((tm, D_MODEL), lambda i: (i, 0)),
        out_shape=jax.ShapeDtypeStruct((t, D_MODEL), F32),
        compiler_params=_cparams(("arbitrary",)),
        name="merge",
    )(ya_t, o_t, g_t, g_t, g_t, x2d, wa_t, wb_t, wo_t, ln_g, ln_b)


SEQ_BLK = 8


def _split3(x):
    hi = x.astype(BF16)
    r1 = x - hi.astype(F32)
    mid = r1.astype(BF16)
    lo = (r1 - mid.astype(F32)).astype(BF16)
    return hi, mid, lo


def _ssd_sample_kernel(xbc_ref, za_ref, dt_ref, cst_ref, cw_ref, dtb_ref, alog_ref, dsk_ref, nw_ref, st_ref,
                       sto_ref, ya_ref,
                       dec_ref, xdt_ref, bm_ref, cmt_ref, xs_ref, y_ref):
    i = pl.program_id(0)
    ns = xbc_ref.shape[0]
    P = SSM_HEAD_DIM
    gs = D_SSM // SSM_GROUPS

    @pl.when(i == 0)
    def _():
        pre = xbc_ref[...]
        acc = pre * cw_ref[CONV_WIDTH - 1:CONV_WIDTH, :] + cw_ref[CONV_WIDTH:CONV_WIDTH + 1, :]
        for w in range(CONV_WIDTH - 1):
            acc = acc + cst_ref[w] * cw_ref[w:w + 1, :]
        xbc = _silu(acc)
        nbc = SSM_GROUPS * SSM_STATE
        xs_t = xbc[:, :D_SSM].T
        xs_ref[...] = xs_t
        bm_ref[...] = xbc[:, D_SSM:D_SSM + nbc]
        cmt_ref[...] = xbc[:, D_SSM + nbc:].T
        dt_t = _softplus(dt_ref[...] + dtb_ref[...])
        a_col = -jnp.exp(alog_ref[...])
        dec_t = jnp.exp(dt_t * a_col)
        for h in range(SSM_HEADS):
            rows = slice(h * P, (h + 1) * P)
            dec_ref[rows, :] = jnp.broadcast_to(dec_t[h:h + 1, :], (P, ns))
            xdt_ref[rows, :] = xs_t[rows] * dt_t[h:h + 1, :]
        y_ref[...] = jnp.zeros_like(y_ref)

    kk = lax.broadcasted_iota(jnp.int32, (ns, SEQ_BLK * SSM_STATE), 0)
    cc = lax.broadcasted_iota(jnp.int32, (ns, SEQ_BLK * SSM_STATE), 1)
    sel = (kk == i * SEQ_BLK + (cc >> 7)).astype(BF16)

    def bsel(x):
        hi, mid, lo = _split3(x)
        d = lambda a: jnp.dot(a, sel, preferred_element_type=F32)
        return d(hi) + d(mid) + d(lo)

    dec_b = bsel(dec_ref[...])
    xdt_b = bsel(xdt_ref[...])
    lane = lax.broadcasted_iota(jnp.int32, (SSM_STATE, ns), 1)
    for r in range(SEQ_BLK):
        n = i * SEQ_BLK + r
        cs = slice(r * SSM_STATE, (r + 1) * SSM_STATE)
        brow = bm_ref[pl.ds(n, 1), :]
        onehot = lane == n
        for g in range(SSM_GROUPS):
            rows = slice(g * gs, (g + 1) * gs)
            bg = brow[:, g * SSM_STATE:(g + 1) * SSM_STATE]
            hn = dec_b[rows, cs] * st_ref[r, rows, :] + xdt_b[rows, cs] * bg
            sto_ref[r, rows, :] = hn
            cg = jnp.where(onehot, cmt_ref[g * SSM_STATE:(g + 1) * SSM_STATE, :], 0.0)
            y_ref[rows, :] += jnp.dot(hn.astype(BF16), cg.astype(BF16), preferred_element_type=F32)

    @pl.when(i == pl.num_programs(0) - 1)
    def _():
        y_t = y_ref[...] + jnp.broadcast_to(dsk_ref[...], y_ref.shape) * xs_ref[...]
        w_b = jnp.broadcast_to(nw_ref[...], y_ref.shape)
        ya_ref[...] = _gated_rmsnorm_t(y_t, za_ref[...].astype(F32), w_b).astype(BF16)


def _ssd_sample(xbc_tok, u_t, g_t, cst, cw_rows, dtb_col, alog_col, dsk_col, nw_col, state):
    ns = u_t.shape[1]
    const2 = lambda i: (0, 0)
    return pl.pallas_call(
        _ssd_sample_kernel,
        grid=(ns // SEQ_BLK,),
        in_specs=[pl.BlockSpec((ns, CONV_DIM), const2),
                  pl.BlockSpec((D_SSM, ns), lambda i: (ROW_ZA // D_SSM, 0)),
                  pl.BlockSpec((LANES, ns), lambda i: (ROW_DT // LANES, 0)),
                  pl.BlockSpec((CONV_WIDTH - 1, ns, CONV_DIM), lambda i: (0, 0, 0)),
                  pl.BlockSpec((CARRY_ROWS, CONV_DIM), const2),
                  pl.BlockSpec((LANES, 1), const2),
                  pl.BlockSpec((LANES, 1), const2),
                  pl.BlockSpec((D_SSM, 1), const2),
                  pl.BlockSpec((D_SSM, 1), const2),
                  pl.BlockSpec((SEQ_BLK, D_SSM, SSM_STATE), lambda i: (i, 0, 0))],
        out_specs=[pl.BlockSpec((SEQ_BLK, D_SSM, SSM_STATE), lambda i: (i, 0, 0)),
                   pl.BlockSpec((D_SSM, ns), const2)],
        out_shape=[jax.ShapeDtypeStruct(state.shape, F32),
                   jax.ShapeDtypeStruct((D_SSM, ns), BF16)],
        scratch_shapes=[pltpu.VMEM((D_SSM, ns), F32),
                        pltpu.VMEM((D_SSM, ns), F32),
                        pltpu.VMEM((ns, SSM_GROUPS * SSM_STATE), F32),
                        pltpu.VMEM((SSM_GROUPS * SSM_STATE, ns), F32),
                        pltpu.VMEM((D_SSM, ns), F32),
                        pltpu.VMEM((D_SSM, ns), F32)],
        compiler_params=_cparams(("arbitrary",)),
        name="ssd_sample",
    )(xbc_tok, g_t, u_t, cst, cw_rows, dtb_col, alog_col, dsk_col, nw_col, state)


def _qprep_sample_kernel(q_ref, cos_ref, sin_ref, qt_ref, qf_ref):
    ns = q_ref.shape[1]
    qf_ref[...] = jnp.zeros_like(qf_ref)
    _rope_heads_t(q_ref, cos_ref[...], sin_ref[...], qf_ref, ATT_SCALE)
    for h in range(ATT_HEADS):
        qt_ref[h] = qf_ref[:, h * ns:(h + 1) * ns].T


def _qprep_sample(u_t, cos_t, sin_t):
    ns = u_t.shape[1]
    return pl.pallas_call(
        _qprep_sample_kernel,
        grid=(1,),
        in_specs=[pl.BlockSpec((D_ATT, ns), lambda i: (ROW_Q // D_ATT, 0)),
                  pl.BlockSpec((HEAD_DIM // 2, ns), lambda i: (0, 0)),
                  pl.BlockSpec((HEAD_DIM // 2, ns), lambda i: (0, 0))],
        out_specs=pl.BlockSpec((ATT_HEADS, ns, D_KV), lambda i: (0, 0, 0)),
        out_shape=jax.ShapeDtypeStruct((ATT_HEADS, ns, D_KV), F32),
        scratch_shapes=[pltpu.VMEM((D_KV, ATT_HEADS * ns), F32)],
        compiler_params=_cparams(("arbitrary",)),
        name="qprep_sample",
    )(u_t, cos_t, sin_t)


def _attn_sample_kernel(pt_ref, q_ref, kn_ref, vn_ref, *refs, n_pages, page):
    sb = q_ref.shape[0]
    k_refs = refs[:sb * n_pages]
    v_refs = refs[sb * n_pages:2 * sb * n_pages]
    o_ref, s_ref = refs[2 * sb * n_pages:]
    nb = n_pages * page // MOBA_BLOCK
    nkeys = n_pages * page
    nh = ATT_HEADS

    s_own = []
    for i in range(sb):
        qf = q_ref[i]
        qb = qf.astype(BF16)
        for j in range(n_pages):
            kp = k_refs[i * n_pages + j][...]
            s_ref[i * nh:(i + 1) * nh, j * page:(j + 1) * page] = jnp.dot(
                qb, kp.astype(BF16), preferred_element_type=F32)
        s_own.append(jnp.sum(qf * kn_ref[i], axis=1, keepdims=True))
    s_own = jnp.concatenate(s_own, axis=0)
    lane = lax.broadcasted_iota(jnp.int32, (sb * nh, LANES), 1)
    gate = jnp.zeros((sb * nh, LANES), F32)
    for b in range(nb):
        bsum = jnp.sum(s_ref[:, b * MOBA_BLOCK:(b + 1) * MOBA_BLOCK], axis=1, keepdims=True)
        gate = jnp.where(lane == b, bsum, gate)
    bias = _top3_mask(gate, float(nb), 1)
    expand = (lax.broadcasted_iota(jnp.int32, (LANES, nkeys), 0)
              == (lax.broadcasted_iota(jnp.int32, (LANES, nkeys), 1) >> int(math.log2(MOBA_BLOCK)))).astype(F32)
    sel_keys = jnp.dot(jnp.where(bias == 0.0, 1.0, 0.0), expand, preferred_element_type=F32)
    s = jnp.where(sel_keys > 0.5, s_ref[...], NEG)
    m = jnp.maximum(jnp.max(s, axis=1, keepdims=True), s_own)
    p = jnp.exp(s - m)
    p_own = jnp.exp(s_own - m)
    l = jnp.sum(p, axis=1, keepdims=True) + p_own
    pb = p.astype(BF16)
    nt = (((1,), (1,)), ((), ()))
    hh = lax.broadcasted_iota(jnp.int32, (nh, D_KV), 0) >> int(math.log2(Q_PER_KV))
    cg = lax.broadcasted_iota(jnp.int32, (nh, D_KV), 1) >> int(math.log2(HEAD_DIM))
    for i in range(sb):
        rows = slice(i * nh, (i + 1) * nh)
        acc = p_own[rows] * vn_ref[i]
        for j in range(n_pages):
            acc = acc + lax.dot_general(pb[rows, j * page:(j + 1) * page],
                                        v_refs[i * n_pages + j][...].astype(BF16), nt,
                                        preferred_element_type=F32)
        o_ref[i] = jnp.where(hh == cg, acc / l[rows], 0.0)


ATT_SEQ_BLK = 4


def _attn_sample(page_table_flat, q_rows, k_new, v_new, ck_t, cv_t, n_pages):
    ns = q_rows.shape[0]
    page = ck_t.shape[2]
    sb = ATT_SEQ_BLK

    def pspec(i, j):
        return pl.BlockSpec((None, D_KV, page), lambda n, pt, i=i, j=j: (pt[(n * sb + i) * n_pages + j], 0, 0))

    pages = [pspec(i, j) for i in range(sb) for j in range(n_pages)]
    row3 = lambda n, pt: (n, 0, 0)
    grid_spec = pltpu.PrefetchScalarGridSpec(
        num_scalar_prefetch=1,
        grid=(ns // sb,),
        in_specs=[pl.BlockSpec((sb, ATT_HEADS, D_KV), row3),
                  pl.BlockSpec((sb, 1, D_KV), row3),
                  pl.BlockSpec((sb, 1, D_KV), row3)] + pages + pages,
        out_specs=pl.BlockSpec((sb, ATT_HEADS, D_KV), row3),
        scratch_shapes=[pltpu.VMEM((sb * ATT_HEADS, n_pages * page), F32)],
    )
    return pl.pallas_call(
        functools.partial(_attn_sample_kernel, n_pages=n_pages, page=page),
        grid_spec=grid_spec,
        out_shape=jax.ShapeDtypeStruct((ns, ATT_HEADS, D_KV), F32),
        compiler_params=_cparams(("arbitrary",)),
        name="attn_sample",
    )(page_table_flat, q_rows, k_new, v_new, *([ck_t] * (sb * n_pages)), *([cv_t] * (sb * n_pages)))


def _ocompact_kernel(o_ref, ot_ref):
    for h in range(ATT_HEADS):
        g = h // Q_PER_KV
        ot_ref[h * HEAD_DIM:(h + 1) * HEAD_DIM, :] = o_ref[h].T[g * HEAD_DIM:(g + 1) * HEAD_DIM, :]


def _ocompact(o_h):
    ns = o_h.shape[1]
    return pl.pallas_call(
        _ocompact_kernel,
        grid=(1,),
        in_specs=[pl.BlockSpec((ATT_HEADS, ns, D_KV), lambda i: (0, 0, 0))],
        out_specs=pl.BlockSpec((D_ATT, ns), lambda i: (0, 0)),
        out_shape=jax.ShapeDtypeStruct((D_ATT, ns), F32),
        compiler_params=_cparams(("arbitrary",)),
        name="o_compact",
    )(o_h)


def _rope_tables(pos):
    half = HEAD_DIM // 2
    inv_freq = jnp.power(ROPE_THETA, -jnp.arange(half, dtype=F32) * 2.0 / HEAD_DIM)
    ang = pos.astype(F32)[:, None] * inv_freq[None, :]
    return jnp.cos(ang).T, jnp.sin(ang).T


def _prep_weights(w_in, conv_w, conv_b, dt_bias, a_log, d_skip, ssm_norm_w, w_a_out, w_b_out, w_out):
    sp = np.cumsum([D_SSM, D_SSM, SSM_GROUPS * SSM_STATE, SSM_GROUPS * SSM_STATE, SSM_HEADS,
                    D_ATT, D_KV, D_KV, D_ATT, D_MODEL])
    z_a, x_a, b_a, c_a, dt, q, k, v, z_b, g_a, g_b = jnp.split(w_in, [int(s) for s in sp], axis=1)
    u_cols = jnp.concatenate([q, dt], axis=1)
    u_cols = jnp.pad(u_cols, ((0, 0), (0, UT_ROWS - u_cols.shape[1])))
    w_x = jnp.concatenate([x_a, b_a, c_a], axis=1).astype(BF16)
    g_cols = jnp.concatenate([z_a, z_b, g_a, g_b], axis=1)
    w_t = jnp.concatenate([jnp.concatenate([u_cols[:, j * UT_TN:(j + 1) * UT_TN],
                                            g_cols[:, j * GT_TN:(j + 1) * GT_TN]], axis=1)
                           for j in range(PROJ_STEPS)], axis=1).T.astype(BF16)
    w_kv = jnp.concatenate([k, v], axis=1).T.astype(BF16)
    cw_rows = jnp.pad(jnp.concatenate([conv_w, conv_b[None, :]], axis=0),
                      ((0, CARRY_ROWS - CONV_WIDTH - 1), (0, 0)))
    pad = LANES - SSM_HEADS
    dtb_col = jnp.pad(dt_bias, (0, pad))[:, None]
    alog_col = jnp.pad(a_log, (0, pad))[:, None]
    dsk_col = jnp.repeat(d_skip, SSM_HEAD_DIM)[:, None]
    nw_col = ssm_norm_w[:, None]
    return (w_t, w_x, w_kv, cw_rows, dtb_col, alog_col, dsk_col, nw_col,
            w_a_out.T.astype(BF16), w_b_out.T.astype(BF16), w_out.T.astype(BF16))


def kernel(x_prompt, x_sample, cache_k, cache_v, state_conv, state_ssm, page_table,
           w_in, conv_w, conv_b, dt_bias, a_log, d_skip, ssm_norm_w, w_a_out, w_b_out, w_out, ln_g, ln_b):
    assert w_in.shape[0] == DEPTH
    n_p, l_p, _ = x_prompt.shape
    n_s, l_s, _ = x_sample.shape
    n_pages, page = page_table.shape[1], cache_k.shape[2]
    past_len = n_pages * page
    assert l_s == 1 and l_p % MOBA_BLOCK == 0 and past_len % MOBA_BLOCK == 0 and MOBA_BLOCK % page == 0
    assert n_s % LANES == 0 and past_len // MOBA_BLOCK >= MOBA_TOPK

    (w_t, w_x, w_kv, cw_rows, dtb_col, alog_col, dsk_col, nw_col, wa_t, wb_t, wo_t) = _prep_weights(
        w_in[0], conv_w[0], conv_b[0], dt_bias[0], a_log[0], d_skip[0], ssm_norm_w[0],
        w_a_out[0], w_b_out[0], w_out[0])
    lng, lnb = ln_g[0][None, :], ln_b[0][None, :]

    xp = x_prompt.reshape(n_p * l_p, D_MODEL)
    cos_t, sin_t = _rope_tables(jnp.arange(l_p, dtype=jnp.int32))
    u_t, g_t, xbc_p, kp_t, vp_t, vt3 = _in_proj(xp, w_t, w_x, w_kv, 1024 if l_p % 1024 == 0 else MOBA_BLOCK,
                                                n_p, l_p)
    rope_tile = 4 * MOBA_BLOCK if l_p % (4 * MOBA_BLOCK) == 0 else MOBA_BLOCK
    kp_rt, k_bf, kmean, knmax = _rope_k(kp_t, cos_t, sin_t, rope_tile, MOBA_BLOCK, BF16)
    ya_t, ssm_p = _ssd_prompt(xbc_p, u_t, g_t, n_p, l_p, cw_rows, dtb_col, alog_col, dsk_col, nw_col)
    nb = l_p // MOBA_BLOCK
    kmean_g = jnp.transpose(kmean.reshape(n_p, nb, KV_HEADS, HEAD_DIM), (0, 2, 1, 3))
    o_t = _attn_prompt(u_t, k_bf, vt3, kmean_g, knmax.reshape(n_p, nb, LANES), cos_t, sin_t, n_p, l_p)
    to_cache = lambda a: jnp.transpose(a.reshape(a.shape[0], KV_HEADS, HEAD_DIM, a.shape[2]), (0, 3, 1, 2))[None]
    tm_merge = 2 * MOBA_BLOCK if (n_p * l_p) % (2 * MOBA_BLOCK) == 0 else MOBA_BLOCK
    y_p = _merge(ya_t, o_t, g_t, xp, wa_t, wb_t, wo_t, lng, lnb, tm_merge)
    conv_p = jnp.stack([xbc_p[(b + 1) * l_p - (CONV_WIDTH - 1):(b + 1) * l_p] for b in range(n_p)])

    xs = x_sample.reshape(n_s, D_MODEL)
    pos_s = jnp.full((n_s,), past_len, dtype=jnp.int32)
    cos_s, sin_s = _rope_tables(pos_s)
    us_t, gs_t, xbc_s, ks_t, vs_t, _ = _in_proj(xs, w_t, w_x, w_kv, n_s, 1, n_s)
    k_s = _rope_k(ks_t, cos_s, sin_s, n_s, n_s, F32)[1]
    v_s = vs_t[0].T
    cst = jnp.transpose(state_conv[0], (1, 0, 2))
    st_in = state_ssm[0].reshape(n_s, D_SSM, SSM_STATE)
    ssm_s, yas_t = _ssd_sample(xbc_s, us_t, gs_t, cst, cw_rows, dtb_col, alog_col, dsk_col, nw_col, st_in)
    q_rows = jnp.transpose(_qprep_sample(us_t, cos_s, sin_s), (1, 0, 2))
    pages_t = lambda c: jnp.transpose(c[0], (0, 2, 3, 1)).reshape(c.shape[1], D_KV, page)
    o_s = _attn_sample(page_table.reshape(-1), q_rows, k_s[:, None, :], v_s[:, None, :],
                       pages_t(cache_k), pages_t(cache_v), n_pages)
    os_t = _ocompact(jnp.transpose(o_s, (1, 0, 2)))
    y_s = _merge(yas_t, os_t, gs_t, xs, wa_t, wb_t, wo_t, lng, lnb, n_s)
    conv_s = jnp.concatenate([state_conv[0][:, 1:], xbc_s[:, None, :]], axis=1)

    return (y_p.reshape(n_p, l_p, D_MODEL), y_s.reshape(n_s, l_s, D_MODEL),
            to_cache(kp_rt), to_cache(vp_t),
            k_s.reshape(1, n_s, l_s, KV_HEADS, HEAD_DIM), v_s.reshape(1, n_s, l_s, KV_HEADS, HEAD_DIM),
            conv_p[None], conv_s[None],
            ssm_p.reshape(1, n_p, SSM_HEADS, SSM_HEAD_DIM, SSM_STATE),
            ssm_s.reshape(1, n_s, SSM_HEADS, SSM_HEAD_DIM, SSM_STATE))
```

```python
import functools
import math

import numpy as np
import jax
import jax.numpy as jnp
from jax import lax
from jax.experimental import pallas as pl
from jax.experimental.pallas import tpu as pltpu

F32 = jnp.float32
BF16 = jnp.bfloat16

D_MODEL = 1024
HEAD_DIM = 64
SSM_HEADS = 16
SSM_HEAD_DIM = 64
SSM_GROUPS = 4
SSM_STATE = 128
CONV_WIDTH = 4
D_SSM = SSM_HEADS * SSM_HEAD_DIM
CONV_DIM = D_SSM + 2 * SSM_GROUPS * SSM_STATE
SSD_CHUNK = 128
ATT_HEADS = 16
KV_HEADS = 4
Q_PER_KV = ATT_HEADS // KV_HEADS
D_ATT = ATT_HEADS * HEAD_DIM
D_KV = KV_HEADS * HEAD_DIM
MOBA_BLOCK = 256
MOBA_TOPK = 3
ATT_SCALE = HEAD_DIM ** -0.5
ROPE_THETA = 10000.0
NORM_EPS = 1e-5
DEPTH = 1
DEEPNORM_ALPHA = (2 * DEPTH) ** 0.25

LANES = 128
VMEM_LIMIT = 56 * 1024 * 1024

ROW_Q = 0
ROW_DT = ROW_Q + D_ATT
UT_ROWS = ROW_DT + LANES
ROW_ZA = 0
ROW_ZB = ROW_ZA + D_SSM
ROW_GA = ROW_ZB + D_ATT
ROW_GB = ROW_GA + D_MODEL
GT_ROWS = ROW_GB + D_MODEL
PROJ_STEPS = 4
UT_TN = UT_ROWS // PROJ_STEPS
GT_TN = GT_ROWS // PROJ_STEPS
XT_TN = CONV_DIM // PROJ_STEPS

NEG = -1e30
HI = lax.Precision.HIGHEST


def _sigmoid(x):
    return 0.5 * jnp.tanh(0.5 * x) + 0.5


def _silu(x):
    return x * _sigmoid(x)


def _softplus(x):
    return jnp.maximum(x, 0.0) + jnp.log(1.0 + jnp.exp(-jnp.abs(x)))


def _cparams(sem):
    return pltpu.CompilerParams(dimension_semantics=sem, vmem_limit_bytes=VMEM_LIMIT)


def _inproj_kernel(x_ref, wt_ref, wx_ref, wkv_ref, ut_ref, gt_ref, xt_ref, k_ref, v_ref, vb_ref):
    x = x_ref[...].astype(BF16)
    nt = (((1,), (1,)), ((), ()))
    res = lax.dot_general(wt_ref[...], x, nt, preferred_element_type=F32)
    ut_ref[...] = res[:UT_TN]
    gt_ref[...] = res[UT_TN:].astype(BF16)
    xt_ref[...] = jnp.dot(x, wx_ref[...], preferred_element_type=F32)

    @pl.when(pl.program_id(1) == 0)
    def _():
        kv = lax.dot_general(wkv_ref[...], x, nt, preferred_element_type=F32)
        k_ref[...] = kv[:D_KV]
        v_ref[...] = kv[D_KV:]
        wb = vb_ref.shape[2]
        for c in range(vb_ref.shape[0]):
            vb_ref[c] = kv[D_KV:, c * wb:(c + 1) * wb].astype(BF16)


def _in_proj(x2d, w_t, w_x, w_kv, tm, n, l):
    t = x2d.shape[0]
    per = l // tm
    wb = min(tm, MOBA_BLOCK)
    kv_spec = pl.BlockSpec((None, D_KV, tm), lambda i, j: (i // per, 0, i % per))
    return pl.pallas_call(
        _inproj_kernel,
        grid=(t // tm, PROJ_STEPS),
        in_specs=[pl.BlockSpec((tm, D_MODEL), lambda i, j: (i, 0)),
                  pl.BlockSpec((UT_TN + GT_TN, D_MODEL), lambda i, j: (j, 0)),
                  pl.BlockSpec((D_MODEL, XT_TN), lambda i, j: (0, j)),
                  pl.BlockSpec((2 * D_KV, D_MODEL), lambda i, j: (0, 0))],
        out_specs=[pl.BlockSpec((UT_TN, tm), lambda i, j: (j, i)),
                   pl.BlockSpec((GT_TN, tm), lambda i, j: (j, i)),
                   pl.BlockSpec((tm, XT_TN), lambda i, j: (i, j)), kv_spec, kv_spec,
                   pl.BlockSpec((tm // wb, D_KV, wb), lambda i, j: (i, 0, 0))],
        out_shape=[jax.ShapeDtypeStruct((UT_ROWS, t), F32),
                   jax.ShapeDtypeStruct((GT_ROWS, t), BF16),
                   jax.ShapeDtypeStruct((t, CONV_DIM), F32),
                   jax.ShapeDtypeStruct((n, D_KV, l), F32),
                   jax.ShapeDtypeStruct((n, D_KV, l), F32),
                   jax.ShapeDtypeStruct((t // wb, D_KV, wb), BF16)],
        compiler_params=_cparams(("arbitrary", "arbitrary")),
        name="in_proj",
    )(x2d, w_t, w_x, w_kv)


def _ropek_kernel(k_ref, cos_ref, sin_ref, ko_ref, kt_ref, km_ref, kn_ref):
    half = HEAD_DIM // 2
    nblk = km_ref.shape[0]
    blk = k_ref.shape[1] // nblk
    grp = (lax.broadcasted_iota(jnp.int32, (D_KV, LANES), 0) >> int(math.log2(HEAD_DIM))
           == lax.broadcasted_iota(jnp.int32, (D_KV, LANES), 1)).astype(F32)
    for i in range(nblk):
        sl = slice(i * blk, (i + 1) * blk)
        cos, sin = cos_ref[:, sl], sin_ref[:, sl]
        rows = []
        for g in range(KV_HEADS):
            x1 = k_ref[g * HEAD_DIM:g * HEAD_DIM + half, sl]
            x2 = k_ref[g * HEAD_DIM + half:(g + 1) * HEAD_DIM, sl]
            rows += [x1 * cos - x2 * sin, x2 * cos + x1 * sin]
        kr = jnp.concatenate(rows, axis=0)
        ko_ref[:, sl] = kr
        kt = kr.T
        kq = kt.astype(kt_ref.dtype)
        kt_ref[sl, :] = kq
        km_ref[i] = jnp.mean(kt, axis=0, keepdims=True)
        kf = kq.astype(F32)
        nrm2 = jnp.dot(kf * kf, grp, preferred_element_type=F32)
        kn_ref[i] = jnp.sqrt(jnp.max(nrm2, axis=0, keepdims=True)) * (1.0 + 2.0 ** -7)


def _rope_k(k_t, cos_t, sin_t, tile, blk, tok_dtype):
    n, _, l = k_t.shape
    nt = l // tile
    per = tile // blk
    return pl.pallas_call(
        _ropek_kernel,
        grid=(n, nt),
        in_specs=[pl.BlockSpec((None, D_KV, tile), lambda b, i: (b, 0, i)),
                  pl.BlockSpec((HEAD_DIM // 2, tile), lambda b, i: (0, i)),
                  pl.BlockSpec((HEAD_DIM // 2, tile), lambda b, i: (0, i))],
        out_specs=[pl.BlockSpec((None, D_KV, tile), lambda b, i: (b, 0, i)),
                   pl.BlockSpec((tile, D_KV), lambda b, i: (b * nt + i, 0)),
                   pl.BlockSpec((per, 1, D_KV), lambda b, i: (b * nt + i, 0, 0)),
                   pl.BlockSpec((per, 1, LANES), lambda b, i: (b * nt + i, 0, 0))],
        out_shape=[jax.ShapeDtypeStruct((n, D_KV, l), F32),
                   jax.ShapeDtypeStruct((n * l, D_KV), tok_dtype),
                   jax.ShapeDtypeStruct((n * nt * per, 1, D_KV), F32),
                   jax.ShapeDtypeStruct((n * nt * per, 1, LANES), F32)],
        compiler_params=_cparams(("arbitrary", "arbitrary")),
        name="rope_k",
    )(k_t, cos_t, sin_t)


def _gated_rmsnorm_t(y_t, z_t, w_b):
    g = y_t * _silu(z_t)
    gsz = D_SSM // SSM_GROUPS
    outs = []
    for i in range(SSM_GROUPS):
        gg = g[i * gsz:(i + 1) * gsz]
        ms = jnp.mean(gg * gg, axis=0, keepdims=True)
        outs.append(gg * lax.rsqrt(ms + NORM_EPS))
    return jnp.concatenate(outs, axis=0) * w_b


CARRY_ROWS = 8


def _ssd_prompt_kernel(xbc_ref, za_ref, dt_ref, cw_ref, dtb_ref, alog_ref, dsk_ref, nw_ref,
                       ya_ref, st_ref, stage_ref, pb_ref):
    c = pl.program_id(1)
    L = SSD_CHUNK

    @pl.when((pl.program_id(0) == 0) & (c == 0))
    def _():
        pb_ref[0] = jnp.broadcast_to(dsk_ref[...], pb_ref.shape[1:])
        pb_ref[1] = jnp.broadcast_to(nw_ref[...], pb_ref.shape[1:])

    @pl.when(c == 0)
    def _():
        stage_ref[:CARRY_ROWS, :] = jnp.zeros((CARRY_ROWS, CONV_DIM), F32)
        st_ref[...] = jnp.zeros_like(st_ref)

    for k in range(xbc_ref.shape[0] // L):
        _ssd_chunk(xbc_ref, za_ref, dt_ref, cw_ref, dtb_ref, alog_ref, ya_ref, st_ref, stage_ref, pb_ref,
                   slice(k * L, (k + 1) * L))


def _ssd_chunk(xbc_ref, za_ref, dt_ref, cw_ref, dtb_ref, alog_ref, ya_ref, st_ref, stage_ref, pb_ref, sl):
    L = SSD_CHUNK
    cur = xbc_ref[sl, :]
    stage_ref[CARRY_ROWS:, :] = cur
    acc = cur * cw_ref[CONV_WIDTH - 1:CONV_WIDTH, :] + cw_ref[CONV_WIDTH:CONV_WIDTH + 1, :]
    for s in range(1, CONV_WIDTH):
        acc = acc + stage_ref[CARRY_ROWS - s:CARRY_ROWS - s + L, :] * cw_ref[CONV_WIDTH - 1 - s:CONV_WIDTH - s, :]
    stage_ref[:CARRY_ROWS, :] = cur[L - CARRY_ROWS:, :]
    xbc = _silu(acc)
    nbc = SSM_GROUPS * SSM_STATE
    xs_t = xbc[:, :D_SSM].T
    bm = xbc[:, D_SSM:D_SSM + nbc]
    cm_t = xbc[:, D_SSM + nbc:].T

    dt_t = _softplus(dt_ref[:, sl] + dtb_ref[...])
    a_col = -jnp.exp(alog_ref[...]) * math.log2(math.e)
    adt_t = dt_t * a_col
    ii = lax.broadcasted_iota(jnp.int32, (L, L), 0)
    jj = lax.broadcasted_iota(jnp.int32, (L, L), 1)
    upper = (ii <= jj).astype(F32)
    lower = (jj <= ii).astype(F32)
    acs_t = jnp.dot(adt_t, upper, precision=HI, preferred_element_type=F32)
    acs = lax.dot_general(lower, adt_t, (((1,), (1,)), ((), ())), precision=HI,
                          preferred_element_type=F32)
    causal_t = ii <= jj

    dsk_b = pb_ref[0]
    ys = []
    R = SSM_HEADS // SSM_GROUPS
    P = SSM_HEAD_DIM
    for g in range(SSM_GROUPS):
        cmg_t = cm_t[g * SSM_STATE:(g + 1) * SSM_STATE]
        bmg = bm[:, g * SSM_STATE:(g + 1) * SSM_STATE].astype(BF16)
        cmg_tb = cmg_t.astype(BF16)
        cb_t = jnp.dot(bmg, cmg_tb, preferred_element_type=F32)
        for r in range(R):
            h = g * R + r
            row = acs_t[h:h + 1, :]
            col = acs[:, h:h + 1]
            dec = jnp.exp2(jnp.where(causal_t, row - col, NEG))
            m_t = (cb_t * dec).astype(BF16)
            xs_h = xs_t[h * P:(h + 1) * P]
            xdt_h = xs_h * dt_t[h:h + 1, :]
            y_diag = jnp.dot(xdt_h.astype(BF16), m_t, preferred_element_type=F32)
            hprev = st_ref[0, h * P:(h + 1) * P, :]
            y_off = jnp.dot(hprev.astype(BF16), cmg_tb, preferred_element_type=F32) * jnp.exp2(row)
            last = acs_t[h:h + 1, L - 1:L]
            te = jnp.exp2(last - row)
            st = jnp.dot((xdt_h * te).astype(BF16), bmg, preferred_element_type=F32)
            st_ref[0, h * P:(h + 1) * P, :] = jnp.exp2(last) * hprev + st
            ys.append(y_diag + y_off + dsk_b[h * P:(h + 1) * P] * xs_h)
    y_t = jnp.concatenate(ys, axis=0)
    ya_ref[:, sl] = _gated_rmsnorm_t(y_t, za_ref[:, sl].astype(F32), pb_ref[1]).astype(BF16)


SSD_CHUNKS_PER_STEP = 2


def _ssd_prompt(xbc_tok, u_t, g_t, n, l, cw_rows, dtb_col, alog_col, dsk_col, nw_col):
    tile = SSD_CHUNK * (SSD_CHUNKS_PER_STEP if l % (SSD_CHUNK * SSD_CHUNKS_PER_STEP) == 0 else 1)
    nc = l // tile
    col = lambda b, c: b * nc + c
    const = lambda b, c: (0, 0)
    return pl.pallas_call(
        _ssd_prompt_kernel,
        grid=(n, nc),
        in_specs=[pl.BlockSpec((tile, CONV_DIM), lambda b, c: (col(b, c), 0)),
                  pl.BlockSpec((D_SSM, tile), lambda b, c: (ROW_ZA // D_SSM, col(b, c))),
                  pl.BlockSpec((LANES, tile), lambda b, c: (ROW_DT // LANES, col(b, c))),
                  pl.BlockSpec((CARRY_ROWS, CONV_DIM), const),
                  pl.BlockSpec((LANES, 1), const),
                  pl.BlockSpec((LANES, 1), const),
                  pl.BlockSpec((D_SSM, 1), const),
                  pl.BlockSpec((D_SSM, 1), const)],
        out_specs=[pl.BlockSpec((D_SSM, tile), lambda b, c: (0, col(b, c))),
                   pl.BlockSpec((1, D_SSM, SSM_STATE), lambda b, c: (b, 0, 0))],
        out_shape=[jax.ShapeDtypeStruct((D_SSM, n * l), BF16),
                   jax.ShapeDtypeStruct((n, D_SSM, SSM_STATE), F32)],
        scratch_shapes=[pltpu.VMEM((CARRY_ROWS + SSD_CHUNK, CONV_DIM), F32),
                        pltpu.VMEM((2, D_SSM, LANES), F32)],
        compiler_params=_cparams(("arbitrary", "arbitrary")),
        name="ssd_prompt",
    )(xbc_tok, g_t, u_t, cw_rows, dtb_col, alog_col, dsk_col, nw_col)


def _rope_heads_t(q_ref, cos, sin, dst_ref, scale):
    nq = q_ref.shape[1]
    half = HEAD_DIM // 2
    for h in range(ATT_HEADS):
        x1 = q_ref[h * HEAD_DIM:h * HEAD_DIM + half, :]
        x2 = q_ref[h * HEAD_DIM + half:(h + 1) * HEAD_DIM, :]
        g = h // Q_PER_KV
        dst_ref[g * HEAD_DIM:g * HEAD_DIM + half, h * nq:(h + 1) * nq] = (x1 * cos - x2 * sin) * scale
        dst_ref[g * HEAD_DIM + half:(g + 1) * HEAD_DIM, h * nq:(h + 1) * nq] = (x2 * cos + x1 * sin) * scale


def _top3_mask(gate, nvalid, axis):
    nb = gate.shape[axis]
    bi = lax.broadcasted_iota(jnp.int32, gate.shape, axis).astype(F32)
    valid = bi < nvalid
    gate = jnp.where(valid, gate, -jnp.inf)
    sel = jnp.zeros(gate.shape, F32)
    for _ in range(MOBA_TOPK):
        mx = jnp.max(gate, axis=axis, keepdims=True)
        idx = jnp.min(jnp.where(gate == mx, bi, float(nb)), axis=axis, keepdims=True)
        pick = bi == idx
        sel = jnp.where(pick, 1.0, sel)
        gate = jnp.where(pick, -jnp.inf, gate)
    return jnp.where((sel > 0.0) & valid, 0.0, NEG)


SUM_ROWS = 16
MAX_EXCESS = 60.0
SCORES_AHEAD = 4


def _attn_prompt_kernel(q_ref, cos_ref, sin_ref, k_ref, vt_ref, km_ref, kn_ref, o_ref,
                        qf_ref, qb_ref, bias_ref, m_ref, acc_ref):
    t = pl.program_id(1)
    nq = MOBA_BLOCK

    @pl.when((pl.program_id(0) == 0) & (t == 0))
    def _():
        qf_ref[...] = jnp.zeros_like(qf_ref)

    _rope_heads_t(q_ref, cos_ref[...], sin_ref[...], qf_ref, ATT_SCALE * math.log2(math.e))
    qb_ref[...] = qf_ref[...].astype(BF16)
    ones = jnp.ones((SUM_ROWS, nq), BF16)
    gw = Q_PER_KV * nq
    qn = []
    for g in range(KV_HEADS):
        qg = qf_ref[g * HEAD_DIM:(g + 1) * HEAD_DIM, g * gw:(g + 1) * gw]
        gate = jnp.dot(km_ref[g], qg, precision=HI, preferred_element_type=F32)
        bias_ref[:, g * gw:(g + 1) * gw] = _top3_mask(gate, t.astype(F32), 0)
        qn.append(jnp.sqrt(jnp.sum(qg * qg, axis=0, keepdims=True)) * (1.0 + 2.0 ** -7))
    causal = lax.broadcasted_iota(jnp.int32, (nq, nq), 0) <= lax.broadcasted_iota(jnp.int32, (nq, nq), 1)

    def run_blocks(blocks, fixed_max=False):
        loaded = []
        for bk, own in blocks:
            kb = k_ref[pl.ds(pl.multiple_of(bk * nq, nq), nq), :]
            brow = None if own else bias_ref[pl.ds(bk, 1), :]
            loaded.append((kb, vt_ref[bk], brow, own))
        items = [(j, h) for j in range(len(blocks)) for h in range(ATT_HEADS)]
        qk = lambda j, h: jnp.dot(loaded[j][0], qb_ref[:, h * nq:(h + 1) * nq], preferred_element_type=F32)
        pending = [qk(*it) for it in items[:SCORES_AHEAD]]
        for idx, (j, h) in enumerate(items):
            s = pending.pop(0)
            if idx + SCORES_AHEAD < len(items):
                pending.append(qk(*items[idx + SCORES_AHEAD]))
            _, vtb, brow, own = loaded[j]
            g = h // Q_PER_KV
            cols = slice(h * nq, (h + 1) * nq)
            if own:
                s = jnp.where(causal, s, NEG)
                m_new = jnp.max(s, axis=0, keepdims=True)
                sub = m_new
            elif fixed_max:
                m_new = None
                sub = jnp.where(brow[:, cols] == 0.0, m_ref[:, cols], -NEG)
            else:
                bh = brow[:, cols]
                m_old = m_ref[:, cols]
                m_new = jnp.maximum(m_old, jnp.max(s, axis=0, keepdims=True) + bh)
                sub = jnp.where(bh == 0.0, m_new, -NEG)
            p = jnp.exp2(s - sub).astype(BF16)
            va = jnp.concatenate([vtb[g * HEAD_DIM:(g + 1) * HEAD_DIM, :], ones], axis=0)
            pv = jnp.dot(va, p, preferred_element_type=F32)
            if own:
                acc_ref[h] = pv
                m_ref[:, cols] = m_new
            elif fixed_max:
                acc_ref[h] += pv
            else:
                acc_ref[h] = acc_ref[h] * jnp.exp2(m_old - m_new) + pv
                m_ref[:, cols] = m_new

    run_blocks([(t, True)])

    bi = lax.broadcasted_iota(jnp.int32, kn_ref.shape, 0)
    kmax = jnp.max(jnp.where(bi < t, kn_ref[...], 0.0), axis=0, keepdims=True)
    excess = [jnp.max(qn[g] * kmax[:, g:g + 1] - m_ref[:, g * gw:(g + 1) * gw]) for g in range(KV_HEADS)]
    fixed_ok = functools.reduce(jnp.maximum, excess) <= MAX_EXCESS
    odd = (t & 1) == 1

    @pl.when(fixed_ok)
    def _():
        def body(i, carry):
            run_blocks([(2 * i, False), (2 * i + 1, False)], fixed_max=True)
            return carry

        lax.fori_loop(0, t >> 1, body, 0)

        @pl.when(odd)
        def _():
            run_blocks([(t - 1, False)], fixed_max=True)

    @pl.when(jnp.logical_not(fixed_ok))
    def _():
        def body(i, carry):
            run_blocks([(2 * i, False), (2 * i + 1, False)])
            return carry

        lax.fori_loop(0, t >> 1, body, 0)

        @pl.when(odd)
        def _():
            run_blocks([(t - 1, False)])

    for h in range(ATT_HEADS):
        o_ref[h * HEAD_DIM:(h + 1) * HEAD_DIM, :] = (
            acc_ref[h, :HEAD_DIM, :] / acc_ref[h, HEAD_DIM:HEAD_DIM + 1, :]).astype(o_ref.dtype)


def _attn_prompt(u_t, k_bf, vt3, kmean, knmax, cos_t, sin_t, n, l):
    nq = MOBA_BLOCK
    nb = l // nq
    col = lambda b, t: b * nb + t
    return pl.pallas_call(
        _attn_prompt_kernel,
        grid=(n, nb),
        in_specs=[pl.BlockSpec((D_ATT, nq), lambda b, t: (ROW_Q // D_ATT, col(b, t))),
                  pl.BlockSpec((HEAD_DIM // 2, nq), lambda b, t: (0, t)),
                  pl.BlockSpec((HEAD_DIM // 2, nq), lambda b, t: (0, t)),
                  pl.BlockSpec((l, D_KV), lambda b, t: (b, 0)),
                  pl.BlockSpec((nb, D_KV, nq), lambda b, t: (b, 0, 0)),
                  pl.BlockSpec((None, KV_HEADS, nb, HEAD_DIM), lambda b, t: (b, 0, 0, 0)),
                  pl.BlockSpec((None, nb, LANES), lambda b, t: (b, 0, 0))],
        out_specs=pl.BlockSpec((D_ATT, nq), lambda b, t: (0, col(b, t))),
        out_shape=jax.ShapeDtypeStruct((D_ATT, n * l), BF16),
        scratch_shapes=[pltpu.VMEM((D_KV, ATT_HEADS * nq), F32),
                        pltpu.VMEM((D_KV, ATT_HEADS * nq), BF16),
                        pltpu.VMEM((nb, ATT_HEADS * nq), F32),
                        pltpu.VMEM((1, ATT_HEADS * nq), F32),
                        pltpu.VMEM((ATT_HEADS, HEAD_DIM + SUM_ROWS, nq), F32)],
        compiler_params=_cparams(("arbitrary", "arbitrary")),
        name="attn_prompt",
    )(u_t, cos_t, sin_t, k_bf, vt3, kmean, knmax)


def _merge_kernel(ya_ref, o_ref, zb_ref, ga_ref, gb_ref, x_ref, wa_ref, wb_ref, wo_ref, lng_ref, lnb_ref,
                  y_ref):
    br_a = jnp.dot(wa_ref[...], ya_ref[...], preferred_element_type=F32)
    ob = (o_ref[...].astype(F32) * _silu(zb_ref[...].astype(F32))).astype(BF16)
    br_b = jnp.dot(wb_ref[...], ob, preferred_element_type=F32)
    mixed = _sigmoid(ga_ref[...].astype(F32)) * br_a + _sigmoid(gb_ref[...].astype(F32)) * br_b
    out_t = jnp.dot(wo_ref[...], mixed.astype(BF16), preferred_element_type=F32)
    hsum = DEEPNORM_ALPHA * x_ref[...] + out_t.T
    mu = jnp.mean(hsum, axis=-1, keepdims=True)
    d = hsum - mu
    var = jnp.mean(d * d, axis=-1, keepdims=True)
    y_ref[...] = d * lax.rsqrt(var + NORM_EPS) * lng_ref[...] + lnb_ref[...]


def _merge(ya_t, o_t, g_t, x2d, wa_t, wb_t, wo_t, ln_g, ln_b, tm):
    t = x2d.shape[0]
    const = lambda i: (0, 0)
    wspec = pl.BlockSpec((D_MODEL, D_MODEL), const, pipeline_mode=pl.Buffered(1))
    return pl.pallas_call(
        _merge_kernel,
        grid=(t // tm,),
        in_specs=[pl.BlockSpec((D_SSM, tm), lambda i: (0, i)),
                  pl.BlockSpec((D_ATT, tm), lambda i: (0, i)),
                  pl.BlockSpec((D_ATT, tm), lambda i: (ROW_ZB // D_ATT, i)),
                  pl.BlockSpec((D_MODEL, tm), lambda i: (ROW_GA // D_MODEL, i)),
                  pl.BlockSpec((D_MODEL, tm), lambda i: (ROW_GB // D_MODEL, i)),
                  pl.BlockSpec((tm, D_MODEL), lambda i: (i, 0)),
                  wspec, wspec, wspec,
                  pl.BlockSpec((1, D_MODEL), const),
                  pl.BlockSpec((1, D_MODEL), const)],
        out_specs=pl.BlockSpec((tm, D_MODEL), lambda i: (i, 0)),
        out_shape=jax.ShapeDtypeStruct((t, D_MODEL), F32),
        compiler_params=_cparams(("arbitrary",)),
        name="merge",
    )(ya_t, o_t, g_t, g_t, g_t, x2d, wa_t, wb_t, wo_t, ln_g, ln_b)


SEQ_BLK = 8


def _split3(x):
    hi = x.astype(BF16)
    r1 = x - hi.astype(F32)
    mid = r1.astype(BF16)
    lo = (r1 - mid.astype(F32)).astype(BF16)
    return hi, mid, lo


def _ssd_sample_kernel(xbc_ref, za_ref, dt_ref, cst_ref, cw_ref, dtb_ref, alog_ref, dsk_ref, nw_ref, st_ref,
                       sto_ref, ya_ref,
                       dec_ref, xdt_ref, bm_ref, cmt_ref, xs_ref, y_ref):
    i = pl.program_id(0)
    ns = xbc_ref.shape[0]
    P = SSM_HEAD_DIM
    gs = D_SSM // SSM_GROUPS

    @pl.when(i == 0)
    def _():
        pre = xbc_ref[...]
        acc = pre * cw_ref[CONV_WIDTH - 1:CONV_WIDTH, :] + cw_ref[CONV_WIDTH:CONV_WIDTH + 1, :]
        for w in range(CONV_WIDTH - 1):
            acc = acc + cst_ref[w] * cw_ref[w:w + 1, :]
        xbc = _silu(acc)
        nbc = SSM_GROUPS * SSM_STATE
        xs_t = xbc[:, :D_SSM].T
        xs_ref[...] = xs_t
        bm_ref[...] = xbc[:, D_SSM:D_SSM + nbc]
        cmt_ref[...] = xbc[:, D_SSM + nbc:].T
        dt_t = _softplus(dt_ref[...] + dtb_ref[...])
        a_col = -jnp.exp(alog_ref[...])
        dec_t = jnp.exp(dt_t * a_col)
        for h in range(SSM_HEADS):
            rows = slice(h * P, (h + 1) * P)
            dec_ref[rows, :] = jnp.broadcast_to(dec_t[h:h + 1, :], (P, ns))
            xdt_ref[rows, :] = xs_t[rows] * dt_t[h:h + 1, :]
        y_ref[...] = jnp.zeros_like(y_ref)

    kk = lax.broadcasted_iota(jnp.int32, (ns, SEQ_BLK * SSM_STATE), 0)
    cc = lax.broadcasted_iota(jnp.int32, (ns, SEQ_BLK * SSM_STATE), 1)
    sel = (kk == i * SEQ_BLK + (cc >> 7)).astype(BF16)

    def bsel(x):
        hi, mid, lo = _split3(x)
        d = lambda a: jnp.dot(a, sel, preferred_element_type=F32)
        return d(hi) + d(mid) + d(lo)

    dec_b = bsel(dec_ref[...])
    xdt_b = bsel(xdt_ref[...])
    lane = lax.broadcasted_iota(jnp.int32, (SSM_STATE, ns), 1)
    for r in range(SEQ_BLK):
        n = i * SEQ_BLK + r
        cs = slice(r * SSM_STATE, (r + 1) * SSM_STATE)
        brow = bm_ref[pl.ds(n, 1), :]
        onehot = lane == n
        for g in range(SSM_GROUPS):
            rows = slice(g * gs, (g + 1) * gs)
            bg = brow[:, g * SSM_STATE:(g + 1) * SSM_STATE]
            hn = dec_b[rows, cs] * st_ref[r, rows, :] + xdt_b[rows, cs] * bg
            sto_ref[r, rows, :] = hn
            cg = jnp.where(onehot, cmt_ref[g * SSM_STATE:(g + 1) * SSM_STATE, :], 0.0)
            y_ref[rows, :] += jnp.dot(hn.astype(BF16), cg.astype(BF16), preferred_element_type=F32)

    @pl.when(i == pl.num_programs(0) - 1)
    def _():
        y_t = y_ref[...] + jnp.broadcast_to(dsk_ref[...], y_ref.shape) * xs_ref[...]
        w_b = jnp.broadcast_to(nw_ref[...], y_ref.shape)
        ya_ref[...] = _gated_rmsnorm_t(y_t, za_ref[...].astype(F32), w_b).astype(BF16)


def _ssd_sample(xbc_tok, u_t, g_t, cst, cw_rows, dtb_col, alog_col, dsk_col, nw_col, state):
    ns = u_t.shape[1]
    const2 = lambda i: (0, 0)
    return pl.pallas_call(
        _ssd_sample_kernel,
        grid=(ns // SEQ_BLK,),
        in_specs=[pl.BlockSpec((ns, CONV_DIM), const2),
                  pl.BlockSpec((D_SSM, ns), lambda i: (ROW_ZA // D_SSM, 0)),
                  pl.BlockSpec((LANES, ns), lambda i: (ROW_DT // LANES, 0)),
                  pl.BlockSpec((CONV_WIDTH - 1, ns, CONV_DIM), lambda i: (0, 0, 0)),
                  pl.BlockSpec((CARRY_ROWS, CONV_DIM), const2),
                  pl.BlockSpec((LANES, 1), const2),
                  pl.BlockSpec((LANES, 1), const2),
                  pl.BlockSpec((D_SSM, 1), const2),
                  pl.BlockSpec((D_SSM, 1), const2),
                  pl.BlockSpec((SEQ_BLK, D_SSM, SSM_STATE), lambda i: (i, 0, 0))],
        out_specs=[pl.BlockSpec((SEQ_BLK, D_SSM, SSM_STATE), lambda i: (i, 0, 0)),
                   pl.BlockSpec((D_SSM, ns), const2)],
        out_shape=[jax.ShapeDtypeStruct(state.shape, F32),
                   jax.ShapeDtypeStruct((D_SSM, ns), BF16)],
        scratch_shapes=[pltpu.VMEM((D_SSM, ns), F32),
                        pltpu.VMEM((D_SSM, ns), F32),
                        pltpu.VMEM((ns, SSM_GROUPS * SSM_STATE), F32),
                        pltpu.VMEM((SSM_GROUPS * SSM_STATE, ns), F32),
                        pltpu.VMEM((D_SSM, ns), F32),
                        pltpu.VMEM((D_SSM, ns), F32)],
        compiler_params=_cparams(("arbitrary",)),
        name="ssd_sample",
    )(xbc_tok, g_t, u_t, cst, cw_rows, dtb_col, alog_col, dsk_col, nw_col, state)


def _qprep_sample_kernel(q_ref, cos_ref, sin_ref, qt_ref, qf_ref):
    ns = q_ref.shape[1]
    qf_ref[...] = jnp.zeros_like(qf_ref)
    _rope_heads_t(q_ref, cos_ref[...], sin_ref[...], qf_ref, ATT_SCALE)
    for h in range(ATT_HEADS):
        qt_ref[h] = qf_ref[:, h * ns:(h + 1) * ns].T


def _qprep_sample(u_t, cos_t, sin_t):
    ns = u_t.shape[1]
    return pl.pallas_call(
        _qprep_sample_kernel,
        grid=(1,),
        in_specs=[pl.BlockSpec((D_ATT, ns), lambda i: (ROW_Q // D_ATT, 0)),
                  pl.BlockSpec((HEAD_DIM // 2, ns), lambda i: (0, 0)),
                  pl.BlockSpec((HEAD_DIM // 2, ns), lambda i: (0, 0))],
        out_specs=pl.BlockSpec((ATT_HEADS, ns, D_KV), lambda i: (0, 0, 0)),
        out_shape=jax.ShapeDtypeStruct((ATT_HEADS, ns, D_KV), F32),
        scratch_shapes=[pltpu.VMEM((D_KV, ATT_HEADS * ns), F32)],
        compiler_params=_cparams(("arbitrary",)),
        name="qprep_sample",
    )(u_t, cos_t, sin_t)


def _attn_sample_kernel(pt_ref, q_ref, kn_ref, vn_ref, *refs, n_pages, page):
    sb = q_ref.shape[0]
    k_refs = refs[:sb * n_pages]
    v_refs = refs[sb * n_pages:2 * sb * n_pages]
    o_ref, s_ref = refs[2 * sb * n_pages:]
    nb = n_pages * page // MOBA_BLOCK
    nkeys = n_pages * page
    nh = ATT_HEADS

    s_own = []
    for i in range(sb):
        qf = q_ref[i]
        qb = qf.astype(BF16)
        for j in range(n_pages):
            kp = k_refs[i * n_pages + j][...]
            s_ref[i * nh:(i + 1) * nh, j * page:(j + 1) * page] = jnp.dot(
                qb, kp.astype(BF16), preferred_element_type=F32)
        s_own.append(jnp.sum(qf * kn_ref[i], axis=1, keepdims=True))
    s_own = jnp.concatenate(s_own, axis=0)
    lane = lax.broadcasted_iota(jnp.int32, (sb * nh, LANES), 1)
    gate = jnp.zeros((sb * nh, LANES), F32)
    for b in range(nb):
        bsum = jnp.sum(s_ref[:, b * MOBA_BLOCK:(b + 1) * MOBA_BLOCK], axis=1, keepdims=True)
        gate = jnp.where(lane == b, bsum, gate)
    bias = _top3_mask(gate, float(nb), 1)
    expand = (lax.broadcasted_iota(jnp.int32, (LANES, nkeys), 0)
              == (lax.broadcasted_iota(jnp.int32, (LANES, nkeys), 1) >> int(math.log2(MOBA_BLOCK)))).astype(F32)
    sel_keys = jnp.dot(jnp.where(bias == 0.0, 1.0, 0.0), expand, preferred_element_type=F32)
    s = jnp.where(sel_keys > 0.5, s_ref[...], NEG)
    m = jnp.maximum(jnp.max(s, axis=1, keepdims=True), s_own)
    p = jnp.exp(s - m)
    p_own = jnp.exp(s_own - m)
    l = jnp.sum(p, axis=1, keepdims=True) + p_own
    pb = p.astype(BF16)
    nt = (((1,), (1,)), ((), ()))
    hh = lax.broadcasted_iota(jnp.int32, (nh, D_KV), 0) >> int(math.log2(Q_PER_KV))
    cg = lax.broadcasted_iota(jnp.int32, (nh, D_KV), 1) >> int(math.log2(HEAD_DIM))
    for i in range(sb):
        rows = slice(i * nh, (i + 1) * nh)
        acc = p_own[rows] * vn_ref[i]
        for j in range(n_pages):
            acc = acc + lax.dot_general(pb[rows, j * page:(j + 1) * page],
                                        v_refs[i * n_pages + j][...].astype(BF16), nt,
                                        preferred_element_type=F32)
        o_ref[i] = jnp.where(hh == cg, acc / l[rows], 0.0)


ATT_SEQ_BLK = 4


def _attn_sample(page_table_flat, q_rows, k_new, v_new, ck_t, cv_t, n_pages):
    ns = q_rows.shape[0]
    page = ck_t.shape[2]
    sb = ATT_SEQ_BLK

    def pspec(i, j):
        return pl.BlockSpec((None, D_KV, page), lambda n, pt, i=i, j=j: (pt[(n * sb + i) * n_pages + j], 0, 0))

    pages = [pspec(i, j) for i in range(sb) for j in range(n_pages)]
    row3 = lambda n, pt: (n, 0, 0)
    grid_spec = pltpu.PrefetchScalarGridSpec(
        num_scalar_prefetch=1,
        grid=(ns // sb,),
        in_specs=[pl.BlockSpec((sb, ATT_HEADS, D_KV), row3),
                  pl.BlockSpec((sb, 1, D_KV), row3),
                  pl.BlockSpec((sb, 1, D_KV), row3)] + pages + pages,
        out_specs=pl.BlockSpec((sb, ATT_HEADS, D_KV), row3),
        scratch_shapes=[pltpu.VMEM((sb * ATT_HEADS, n_pages * page), F32)],
    )
    return pl.pallas_call(
        functools.partial(_attn_sample_kernel, n_pages=n_pages, page=page),
        grid_spec=grid_spec,
        out_shape=jax.ShapeDtypeStruct((ns, ATT_HEADS, D_KV), F32),
        compiler_params=_cparams(("arbitrary",)),
        name="attn_sample",
    )(page_table_flat, q_rows, k_new, v_new, *([ck_t] * (sb * n_pages)), *([cv_t] * (sb * n_pages)))


def _ocompact_kernel(o_ref, ot_ref):
    for h in range(ATT_HEADS):
        g = h // Q_PER_KV
        ot_ref[h * HEAD_DIM:(h + 1) * HEAD_DIM, :] = o_ref[h].T[g * HEAD_DIM:(g + 1) * HEAD_DIM, :]


def _ocompact(o_h):
    ns = o_h.shape[1]
    return pl.pallas_call(
        _ocompact_kernel,
        grid=(1,),
        in_specs=[pl.BlockSpec((ATT_HEADS, ns, D_KV), lambda i: (0, 0, 0))],
        out_specs=pl.BlockSpec((D_ATT, ns), lambda i: (0, 0)),
        out_shape=jax.ShapeDtypeStruct((D_ATT, ns), F32),
        compiler_params=_cparams(("arbitrary",)),
        name="o_compact",
    )(o_h)


def _rope_tables(pos):
    half = HEAD_DIM // 2
    inv_freq = jnp.power(ROPE_THETA, -jnp.arange(half, dtype=F32) * 2.0 / HEAD_DIM)
    ang = pos.astype(F32)[:, None] * inv_freq[None, :]
    return jnp.cos(ang).T, jnp.sin(ang).T


def _prep_weights(w_in, conv_w, conv_b, dt_bias, a_log, d_skip, ssm_norm_w, w_a_out, w_b_out, w_out):
    sp = np.cumsum([D_SSM, D_SSM, SSM_GROUPS * SSM_STATE, SSM_GROUPS * SSM_STATE, SSM_HEADS,
                    D_ATT, D_KV, D_KV, D_ATT, D_MODEL])
    z_a, x_a, b_a, c_a, dt, q, k, v, z_b, g_a, g_b = jnp.split(w_in, [int(s) for s in sp], axis=1)
    u_cols = jnp.concatenate([q, dt], axis=1)
    u_cols = jnp.pad(u_cols, ((0, 0), (0, UT_ROWS - u_cols.shape[1])))
    w_x = jnp.concatenate([x_a, b_a, c_a], axis=1).astype(BF16)
    g_cols = jnp.concatenate([z_a, z_b, g_a, g_b], axis=1)
    w_t = jnp.concatenate([jnp.concatenate([u_cols[:, j * UT_TN:(j + 1) * UT_TN],
                                            g_cols[:, j * GT_TN:(j + 1) * GT_TN]], axis=1)
                           for j in range(PROJ_STEPS)], axis=1).T.astype(BF16)
    w_kv = jnp.concatenate([k, v], axis=1).T.astype(BF16)
    cw_rows = jnp.pad(jnp.concatenate([conv_w, conv_b[None, :]], axis=0),
                      ((0, CARRY_ROWS - CONV_WIDTH - 1), (0, 0)))
    pad = LANES - SSM_HEADS
    dtb_col = jnp.pad(dt_bias, (0, pad))[:, None]
    alog_col = jnp.pad(a_log, (0, pad))[:, None]
    dsk_col = jnp.repeat(d_skip, SSM_HEAD_DIM)[:, None]
    nw_col = ssm_norm_w[:, None]
    return (w_t, w_x, w_kv, cw_rows, dtb_col, alog_col, dsk_col, nw_col,
            w_a_out.T.astype(BF16), w_b_out.T.astype(BF16), w_out.T.astype(BF16))


def kernel(x_prompt, x_sample, cache_k, cache_v, state_conv, state_ssm, page_table,
           w_in, conv_w, conv_b, dt_bias, a_log, d_skip, ssm_norm_w, w_a_out, w_b_out, w_out, ln_g, ln_b):
    assert w_in.shape[0] == DEPTH
    n_p, l_p, _ = x_prompt.shape
    n_s, l_s, _ = x_sample.shape
    n_pages, page = page_table.shape[1], cache_k.shape[2]
    past_len = n_pages * page
    assert l_s == 1 and l_p % MOBA_BLOCK == 0 and past_len % MOBA_BLOCK == 0 and MOBA_BLOCK % page == 0
    assert n_s % LANES == 0 and past_len // MOBA_BLOCK >= MOBA_TOPK

    (w_t, w_x, w_kv, cw_rows, dtb_col, alog_col, dsk_col, nw_col, wa_t, wb_t, wo_t) = _prep_weights(
        w_in[0], conv_w[0], conv_b[0], dt_bias[0], a_log[0], d_skip[0], ssm_norm_w[0],
        w_a_out[0], w_b_out[0], w_out[0])
    lng, lnb = ln_g[0][None, :], ln_b[0][None, :]

    xp = x_prompt.reshape(n_p * l_p, D_MODEL)
    cos_t, sin_t = _rope_tables(jnp.arange(l_p, dtype=jnp.int32))
    u_t, g_t, xbc_p, kp_t, vp_t, vt3 = _in_proj(xp, w_t, w_x, w_kv, 1024 if l_p % 1024 == 0 else MOBA_BLOCK,
                                                n_p, l_p)
    rope_tile = 4 * MOBA_BLOCK if l_p % (4 * MOBA_BLOCK) == 0 else MOBA_BLOCK
    kp_rt, k_bf, kmean, knmax = _rope_k(kp_t, cos_t, sin_t, rope_tile, MOBA_BLOCK, BF16)
    ya_t, ssm_p = _ssd_prompt(xbc_p, u_t, g_t, n_p, l_p, cw_rows, dtb_col, alog_col, dsk_col, nw_col)
    nb = l_p // MOBA_BLOCK
    kmean_g = jnp.transpose(kmean.reshape(n_p, nb, KV_HEADS, HEAD_DIM), (0, 2, 1, 3))
    o_t = _attn_prompt(u_t, k_bf, vt3, kmean_g, knmax.reshape(n_p, nb, LANES), cos_t, sin_t, n_p, l_p)
    to_cache = lambda a: jnp.transpose(a.reshape(a.shape[0], KV_HEADS, HEAD_DIM, a.shape[2]), (0, 3, 1, 2))[None]
    tm_merge = 2 * MOBA_BLOCK if (n_p * l_p) % (2 * MOBA_BLOCK) == 0 else MOBA_BLOCK
    y_p = _merge(ya_t, o_t, g_t, xp, wa_t, wb_t, wo_t, lng, lnb, tm_merge)
    conv_p = jnp.stack([xbc_p[(b + 1) * l_p - (CONV_WIDTH - 1):(b + 1) * l_p] for b in range(n_p)])

    xs = x_sample.reshape(n_s, D_MODEL)
    pos_s = jnp.full((n_s,), past_len, dtype=jnp.int32)
    cos_s, sin_s = _rope_tables(pos_s)
    us_t, gs_t, xbc_s, ks_t, vs_t, _ = _in_proj(xs, w_t, w_x, w_kv, n_s, 1, n_s)
    k_s = _rope_k(ks_t, cos_s, sin_s, n_s, n_s, F32)[1]
    v_s = vs_t[0].T
    cst = jnp.transpose(state_conv[0], (1, 0, 2))
    st_in = state_ssm[0].reshape(n_s, D_SSM, SSM_STATE)
    ssm_s, yas_t = _ssd_sample(xbc_s, us_t, gs_t, cst, cw_rows, dtb_col, alog_col, dsk_col, nw_col, st_in)
    q_rows = jnp.transpose(_qprep_sample(us_t, cos_s, sin_s), (1, 0, 2))
    pages_t = lambda c: jnp.transpose(c[0], (0, 2, 3, 1)).reshape(c.shape[1], D_KV, page)
    o_s = _attn_sample(page_table.reshape(-1), q_rows, k_s[:, None, :], v_s[:, None, :],
                       pages_t(cache_k), pages_t(cache_v), n_pages)
    os_t = _ocompact(jnp.transpose(o_s, (1, 0, 2)))
    y_s = _merge(yas_t, os_t, gs_t, xs, wa_t, wb_t, wo_t, lng, lnb, n_s)
    conv_s = jnp.concatenate([state_conv[0][:, 1:], xbc_s[:, None, :]], axis=1)

    return (y_p.reshape(n_p, l_p, D_MODEL), y_s.reshape(n_s, l_s, D_MODEL),
            to_cache(kp_rt), to_cache(vp_t),
            k_s.reshape(1, n_s, l_s, KV_HEADS, HEAD_DIM), v_s.reshape(1, n_s, l_s, KV_HEADS, HEAD_DIM),
            conv_p[None], conv_s[None],
            ssm_p.reshape(1, n_p, SSM_HEADS, SSM_HEAD_DIM, SSM_STATE),
            ssm_s.reshape(1, n_s, SSM_HEADS, SSM_HEAD_DIM, SSM_STATE))
```

```python
import functools
import math

import numpy as np
import jax
import jax.numpy as jnp
from jax import lax
from jax.experimental import pallas as pl
from jax.experimental.pallas import tpu as pltpu

F32 = jnp.float32
BF16 = jnp.bfloat16

D_MODEL = 1024
HEAD_DIM = 64
SSM_HEADS = 16
SSM_HEAD_DIM = 64
SSM_GROUPS = 4
SSM_STATE = 128
CONV_WIDTH = 4
D_SSM = SSM_HEADS * SSM_HEAD_DIM
CONV_DIM = D_SSM + 2 * SSM_GROUPS * SSM_STATE
SSD_CHUNK = 128
ATT_HEADS = 16
KV_HEADS = 4
Q_PER_KV = ATT_HEADS // KV_HEADS
D_ATT = ATT_HEADS * HEAD_DIM
D_KV = KV_HEADS * HEAD_DIM
MOBA_BLOCK = 256
MOBA_TOPK = 3
ATT_SCALE = HEAD_DIM ** -0.5
ROPE_THETA = 10000.0
NORM_EPS = 1e-5
DEPTH = 1
DEEPNORM_ALPHA = (2 * DEPTH) ** 0.25

LANES = 128
VMEM_LIMIT = 56 * 1024 * 1024

ROW_Q = 0
ROW_DT = ROW_Q + D_ATT
UT_ROWS = ROW_DT + LANES
ROW_ZA = 0
ROW_ZB = ROW_ZA + D_SSM
ROW_GA = ROW_ZB + D_ATT
ROW_GB = ROW_GA + D_MODEL
GT_ROWS = ROW_GB + D_MODEL
PROJ_STEPS = 4
UT_TN = UT_ROWS // PROJ_STEPS
GT_TN = GT_ROWS // PROJ_STEPS
XT_TN = CONV_DIM // PROJ_STEPS

NEG = -1e30
HI = lax.Precision.HIGHEST


def _sigmoid(x):
    return 0.5 * jnp.tanh(0.5 * x) + 0.5


def _silu(x):
    return x * _sigmoid(x)


def _softplus(x):
    return jnp.maximum(x, 0.0) + jnp.log(1.0 + jnp.exp(-jnp.abs(x)))


def _cparams(sem):
    return pltpu.CompilerParams(dimension_semantics=sem, vmem_limit_bytes=VMEM_LIMIT)


def _inproj_kernel(x_ref, wt_ref, wx_ref, wkv_ref, ut_ref, gt_ref, xt_ref, k_ref, v_ref, vb_ref):
    x = x_ref[...].astype(BF16)
    nt = (((1,), (1,)), ((), ()))
    res = lax.dot_general(wt_ref[...], x, nt, preferred_element_type=F32)
    ut_ref[...] = res[:UT_TN]
    gt_ref[...] = res[UT_TN:].astype(BF16)
    xt_ref[...] = jnp.dot(x, wx_ref[...], preferred_element_type=F32)

    @pl.when(pl.program_id(1) == 0)
    def _():
        kv = lax.dot_general(wkv_ref[...], x, nt, preferred_element_type=F32)
        k_ref[...] = kv[:D_KV]
        v_ref[...] = kv[D_KV:]
        wb = vb_ref.shape[2]
        for c in range(vb_ref.shape[0]):
            vb_ref[c] = kv[D_KV:, c * wb:(c + 1) * wb].astype(BF16)


def _in_proj(x2d, w_t, w_x, w_kv, tm, n, l):
    t = x2d.shape[0]
    per = l // tm
    wb = min(tm, MOBA_BLOCK)
    kv_spec = pl.BlockSpec((None, D_KV, tm), lambda i, j: (i // per, 0, i % per))
    return pl.pallas_call(
        _inproj_kernel,
        grid=(t // tm, PROJ_STEPS),
        in_specs=[pl.BlockSpec((tm, D_MODEL), lambda i, j: (i, 0)),
                  pl.BlockSpec((UT_TN + GT_TN, D_MODEL), lambda i, j: (j, 0)),
                  pl.BlockSpec((D_MODEL, XT_TN), lambda i, j: (0, j)),
                  pl.BlockSpec((2 * D_KV, D_MODEL), lambda i, j: (0, 0))],
        out_specs=[pl.BlockSpec((UT_TN, tm), lambda i, j: (j, i)),
                   pl.BlockSpec((GT_TN, tm), lambda i, j: (j, i)),
                   pl.BlockSpec((tm, XT_TN), lambda i, j: (i, j)), kv_spec, kv_spec,
                   pl.BlockSpec((tm // wb, D_KV, wb), lambda i, j: (i, 0, 0))],
        out_shape=[jax.ShapeDtypeStruct((UT_ROWS, t), F32),
                   jax.ShapeDtypeStruct((GT_ROWS, t), BF16),
                   jax.ShapeDtypeStruct((t, CONV_DIM), F32),
                   jax.ShapeDtypeStruct((n, D_KV, l), F32),
                   jax.ShapeDtypeStruct((n, D_KV, l), F32),
                   jax.ShapeDtypeStruct((t // wb, D_KV, wb), BF16)],
        compiler_params=_cparams(("arbitrary", "arbitrary")),
        name="in_proj",
    )(x2d, w_t, w_x, w_kv)


def _ropek_kernel(k_ref, cos_ref, sin_ref, ko_ref, kt_ref, km_ref, kn_ref):
    half = HEAD_DIM // 2
    nblk = km_ref.shape[0]
    blk = k_ref.shape[1] // nblk
    grp = (lax.broadcasted_iota(jnp.int32, (D_KV, LANES), 0) >> int(math.log2(HEAD_DIM))
           == lax.broadcasted_iota(jnp.int32, (D_KV, LANES), 1)).astype(F32)
    for i in range(nblk):
        sl = slice(i * blk, (i + 1) * blk)
        cos, sin = cos_ref[:, sl], sin_ref[:, sl]
        rows = []
        for g in range(KV_HEADS):
            x1 = k_ref[g * HEAD_DIM:g * HEAD_DIM + half, sl]
            x2 = k_ref[g * HEAD_DIM + half:(g + 1) * HEAD_DIM, sl]
            rows += [x1 * cos - x2 * sin, x2 * cos + x1 * sin]
        kr = jnp.concatenate(rows, axis=0)
        ko_ref[:, sl] = kr
        kt = kr.T
        kq = kt.astype(kt_ref.dtype)
        kt_ref[sl, :] = kq
        km_ref[i] = jnp.mean(kt, axis=0, keepdims=True)
        kf = kq.astype(F32)
        nrm2 = jnp.dot(kf * kf, grp, preferred_element_type=F32)
        kn_ref[i] = jnp.sqrt(jnp.max(nrm2, axis=0, keepdims=True)) * (1.0 + 2.0 ** -7)


def _rope_k(k_t, cos_t, sin_t, tile, blk, tok_dtype):
    n, _, l = k_t.shape
    nt = l // tile
    per = tile // blk
    return pl.pallas_call(
        _ropek_kernel,
        grid=(n, nt),
        in_specs=[pl.BlockSpec((None, D_KV, tile), lambda b, i: (b, 0, i)),
                  pl.BlockSpec((HEAD_DIM // 2, tile), lambda b, i: (0, i)),
                  pl.BlockSpec((HEAD_DIM // 2, tile), lambda b, i: (0, i))],
        out_specs=[pl.BlockSpec((None, D_KV, tile), lambda b, i: (b, 0, i)),
                   pl.BlockSpec((tile, D_KV), lambda b, i: (b * nt + i, 0)),
                   pl.BlockSpec((per, 1, D_KV), lambda b, i: (b * nt + i, 0, 0)),
                   pl.BlockSpec((per, 1, LANES), lambda b, i: (b * nt + i, 0, 0))],
        out_shape=[jax.ShapeDtypeStruct((n, D_KV, l), F32),
                   jax.ShapeDtypeStruct((n * l, D_KV), tok_dtype),
                   jax.ShapeDtypeStruct((n * nt * per, 1, D_KV), F32),
                   jax.ShapeDtypeStruct((n * nt * per, 1, LANES), F32)],
        compiler_params=_cparams(("arbitrary", "arbitrary")),
        name="rope_k",
    )(k_t, cos_t, sin_t)


def _gated_rmsnorm_t(y_t, z_t, w_b):
    g = y_t * _silu(z_t)
    gsz = D_SSM // SSM_GROUPS
    outs = []
    for i in range(SSM_GROUPS):
        gg = g[i * gsz:(i + 1) * gsz]
        ms = jnp.mean(gg * gg, axis=0, keepdims=True)
        outs.append(gg * lax.rsqrt(ms + NORM_EPS))
    return jnp.concatenate(outs, axis=0) * w_b


CARRY_ROWS = 8


def _ssd_prompt_kernel(xbc_ref, za_ref, dt_ref, cw_ref, dtb_ref, alog_ref, dsk_ref, nw_ref,
                       ya_ref, st_ref, stage_ref, pb_ref):
    c = pl.program_id(1)
    L = SSD_CHUNK

    @pl.when((pl.program_id(0) == 0) & (c == 0))
    def _():
        pb_ref[0] = jnp.broadcast_to(dsk_ref[...], pb_ref.shape[1:])
        pb_ref[1] = jnp.broadcast_to(nw_ref[...], pb_ref.shape[1:])

    @pl.when(c == 0)
    def _():
        stage_ref[:CARRY_ROWS, :] = jnp.zeros((CARRY_ROWS, CONV_DIM), F32)
        st_ref[...] = jnp.zeros_like(st_ref)

    for k in range(xbc_ref.shape[0] // L):
        _ssd_chunk(xbc_ref, za_ref, dt_ref, cw_ref, dtb_ref, alog_ref, ya_ref, st_ref, stage_ref, pb_ref,
                   slice(k * L, (k + 1) * L))


def _ssd_chunk(xbc_ref, za_ref, dt_ref, cw_ref, dtb_ref, alog_ref, ya_ref, st_ref, stage_ref, pb_ref, sl):
    L = SSD_CHUNK
    cur = xbc_ref[sl, :]
    stage_ref[CARRY_ROWS:, :] = cur
    acc = cur * cw_ref[CONV_WIDTH - 1:CONV_WIDTH, :] + cw_ref[CONV_WIDTH:CONV_WIDTH + 1, :]
    for s in range(1, CONV_WIDTH):
        acc = acc + stage_ref[CARRY_ROWS - s:CARRY_ROWS - s + L, :] * cw_ref[CONV_WIDTH - 1 - s:CONV_WIDTH - s, :]
    stage_ref[:CARRY_ROWS, :] = cur[L - CARRY_ROWS:, :]
    xbc = _silu(acc)
    nbc = SSM_GROUPS * SSM_STATE
    xs_t = xbc[:, :D_SSM].T
    bm = xbc[:, D_SSM:D_SSM + nbc]
    cm_t = xbc[:, D_SSM + nbc:].T

    dt_t = _softplus(dt_ref[:, sl] + dtb_ref[...])
    a_col = -jnp.exp(alog_ref[...]) * math.log2(math.e)
    adt_t = dt_t * a_col
    ii = lax.broadcasted_iota(jnp.int32, (L, L), 0)
    jj = lax.broadcasted_iota(jnp.int32, (L, L), 1)
    upper = (ii <= jj).astype(F32)
    lower = (jj <= ii).astype(F32)
    acs_t = jnp.dot(adt_t, upper, precision=HI, preferred_element_type=F32)
    acs = lax.dot_general(lower, adt_t, (((1,), (1,)), ((), ())), precision=HI,
                          preferred_element_type=F32)
    causal_t = ii <= jj

    dsk_b = pb_ref[0]
    ys = []
    R = SSM_HEADS // SSM_GROUPS
    P = SSM_HEAD_DIM
    for g in range(SSM_GROUPS):
        cmg_t = cm_t[g * SSM_STATE:(g + 1) * SSM_STATE]
        bmg = bm[:, g * SSM_STATE:(g + 1) * SSM_STATE].astype(BF16)
        cmg_tb = cmg_t.astype(BF16)
        cb_t = jnp.dot(bmg, cmg_tb, preferred_element_type=F32)
        for r in range(R):
            h = g * R + r
            row = acs_t[h:h + 1, :]
            col = acs[:, h:h + 1]
            dec = jnp.exp2(jnp.where(causal_t, row - col, NEG))
            m_t = (cb_t * dec).astype(BF16)
            xs_h = xs_t[h * P:(h + 1) * P]
            xdt_h = xs_h * dt_t[h:h + 1, :]
            y_diag = jnp.dot(xdt_h.astype(BF16), m_t, preferred_element_type=F32)
            hprev = st_ref[0, h * P:(h + 1) * P, :]
            y_off = jnp.dot(hprev.astype(BF16), cmg_tb, preferred_element_type=F32) * jnp.exp2(row)
            last = acs_t[h:h + 1, L - 1:L]
            te = jnp.exp2(last - row)
            st = jnp.dot((xdt_h * te).astype(BF16), bmg, preferred_element_type=F32)
            st_ref[0, h * P:(h + 1) * P, :] = jnp.exp2(last) * hprev + st
            ys.append(y_diag + y_off + dsk_b[h * P:(h + 1) * P] * xs_h)
    y_t = jnp.concatenate(ys, axis=0)
    ya_ref[:, sl] = _gated_rmsnorm_t(y_t, za_ref[:, sl].astype(F32), pb_ref[1]).astype(BF16)


SSD_CHUNKS_PER_STEP = 2


def _ssd_prompt(xbc_tok, u_t, g_t, n, l, cw_rows, dtb_col, alog_col, dsk_col, nw_col):
    tile = SSD_CHUNK * (SSD_CHUNKS_PER_STEP if l % (SSD_CHUNK * SSD_CHUNKS_PER_STEP) == 0 else 1)
    nc = l // tile
    col = lambda b, c: b * nc + c
    const = lambda b, c: (0, 0)
    return pl.pallas_call(
        _ssd_prompt_kernel,
        grid=(n, nc),
        in_specs=[pl.BlockSpec((tile, CONV_DIM), lambda b, c: (col(b, c), 0)),
                  pl.BlockSpec((D_SSM, tile), lambda b, c: (ROW_ZA // D_SSM, col(b, c))),
                  pl.BlockSpec((LANES, tile), lambda b, c: (ROW_DT // LANES, col(b, c))),
                  pl.BlockSpec((CARRY_ROWS, CONV_DIM), const),
                  pl.BlockSpec((LANES, 1), const),
                  pl.BlockSpec((LANES, 1), const),
                  pl.BlockSpec((D_SSM, 1), const),
                  pl.BlockSpec((D_SSM, 1), const)],
        out_specs=[pl.BlockSpec((D_SSM, tile), lambda b, c: (0, col(b, c))),
                   pl.BlockSpec((1, D_SSM, SSM_STATE), lambda b, c: (b, 0, 0))],
        out_shape=[jax.ShapeDtypeStruct((D_SSM, n * l), BF16),
                   jax.ShapeDtypeStruct((n, D_SSM, SSM_STATE), F32)],
        scratch_shapes=[pltpu.VMEM((CARRY_ROWS + SSD_CHUNK, CONV_DIM), F32),
                        pltpu.VMEM((2, D_SSM, LANES), F32)],
        compiler_params=_cparams(("arbitrary", "arbitrary")),
        name="ssd_prompt",
    )(xbc_tok, g_t, u_t, cw_rows, dtb_col, alog_col, dsk_col, nw_col)


def _rope_heads_t(q_ref, cos, sin, dst_ref, scale):
    nq = q_ref.shape[1]
    half = HEAD_DIM // 2
    for h in range(ATT_HEADS):
        x1 = q_ref[h * HEAD_DIM:h * HEAD_DIM + half, :]
        x2 = q_ref[h * HEAD_DIM + half:(h + 1) * HEAD_DIM, :]
        g = h // Q_PER_KV
        dst_ref[g * HEAD_DIM:g * HEAD_DIM + half, h * nq:(h + 1) * nq] = (x1 * cos - x2 * sin) * scale
        dst_ref[g * HEAD_DIM + half:(g + 1) * HEAD_DIM, h * nq:(h + 1) * nq] = (x2 * cos + x1 * sin) * scale


def _top3_mask(gate, nvalid, axis):
    nb = gate.shape[axis]
    bi = lax.broadcasted_iota(jnp.int32, gate.shape, axis).astype(F32)
    valid = bi < nvalid
    gate = jnp.where(valid, gate, -jnp.inf)
    sel = jnp.zeros(gate.shape, F32)
    for _ in range(MOBA_TOPK):
        mx = jnp.max(gate, axis=axis, keepdims=True)
        idx = jnp.min(jnp.where(gate == mx, bi, float(nb)), axis=axis, keepdims=True)
        pick = bi == idx
        sel = jnp.where(pick, 1.0, sel)
        gate = jnp.where(pick, -jnp.inf, gate)
    return jnp.where((sel > 0.0) & valid, 0.0, NEG)


SUM_ROWS = 16
MAX_EXCESS = 60.0
SCORES_AHEAD = 4


def _attn_prompt_kernel(q_ref, cos_ref, sin_ref, k_ref, vt_ref, km_ref, kn_ref, o_ref,
                        qf_ref, qb_ref, bias_ref, m_ref, acc_ref):
    t = pl.program_id(1)
    nq = MOBA_BLOCK

    @pl.when((pl.program_id(0) == 0) & (t == 0))
    def _():
        qf_ref[...] = jnp.zeros_like(qf_ref)

    _rope_heads_t(q_ref, cos_ref[...], sin_ref[...], qf_ref, ATT_SCALE * math.log2(math.e))
    qb_ref[...] = qf_ref[...].astype(BF16)
    ones = jnp.ones((SUM_ROWS, nq), BF16)
    gw = Q_PER_KV * nq
    qn = []
    for g in range(KV_HEADS):
        qg = qf_ref[g * HEAD_DIM:(g + 1) * HEAD_DIM, g * gw:(g + 1) * gw]
        gate = jnp.dot(km_ref[g], qg, precision=HI, preferred_element_type=F32)
        bias_ref[:, g * gw:(g + 1) * gw] = _top3_mask(gate, t.astype(F32), 0)
        qn.append(jnp.sqrt(jnp.sum(qg * qg, axis=0, keepdims=True)) * (1.0 + 2.0 ** -7))
    causal = lax.broadcasted_iota(jnp.int32, (nq, nq), 0) <= lax.broadcasted_iota(jnp.int32, (nq, nq), 1)

    def run_blocks(blocks, fixed_max=False):
        loaded = []
        for bk, own in blocks:
            kb = k_ref[pl.ds(pl.multiple_of(bk * nq, nq), nq), :]
            brow = None if own else bias_ref[pl.ds(bk, 1), :]
            loaded.append((kb, vt_ref[bk], brow, own))
        items = [(j, h) for j in range(len(blocks)) for h in range(ATT_HEADS)]
        qk = lambda j, h: jnp.dot(loaded[j][0], qb_ref[:, h * nq:(h + 1) * nq], preferred_element_type=F32)
        pending = [qk(*it) for it in items[:SCORES_AHEAD]]
        for idx, (j, h) in enumerate(items):
            s = pending.pop(0)
            if idx + SCORES_AHEAD < len(items):
                pending.append(qk(*items[idx + SCORES_AHEAD]))
            _, vtb, brow, own = loaded[j]
            g = h // Q_PER_KV
            cols = slice(h * nq, (h + 1) * nq)
            if own:
                s = jnp.where(causal, s, NEG)
                m_new = jnp.max(s, axis=0, keepdims=True)
                sub = m_new
            elif fixed_max:
                m_new = None
                sub = jnp.where(brow[:, cols] == 0.0, m_ref[:, cols], -NEG)
            else:
                bh = brow[:, cols]
                m_old = m_ref[:, cols]
                m_new = jnp.maximum(m_old, jnp.max(s, axis=0, keepdims=True) + bh)
                sub = jnp.where(bh == 0.0, m_new, -NEG)
            p = jnp.exp2(s - sub).astype(BF16)
            va = jnp.concatenate([vtb[g * HEAD_DIM:(g + 1) * HEAD_DIM, :], ones], axis=0)
            pv = jnp.dot(va, p, preferred_element_type=F32)
            if own:
                acc_ref[h] = pv
                m_ref[:, cols] = m_new
            elif fixed_max:
                acc_ref[h] += pv
            else:
                acc_ref[h] = acc_ref[h] * jnp.exp2(m_old - m_new) + pv
                m_ref[:, cols] = m_new

    run_blocks([(t, True)])

    bi = lax.broadcasted_iota(jnp.int32, kn_ref.shape, 0)
    kmax = jnp.max(jnp.where(bi < t, kn_ref[...], 0.0), axis=0, keepdims=True)
    excess = [jnp.max(qn[g] * kmax[:, g:g + 1] - m_ref[:, g * gw:(g + 1) * gw]) for g in range(KV_HEADS)]
    fixed_ok = functools.reduce(jnp.maximum, excess) <= MAX_EXCESS
    odd = (t & 1) == 1

    @pl.when(fixed_ok)
    def _():
        def body(i, carry):
            run_blocks([(4 * i + j, False) for j in range(4)], fixed_max=True)
            return carry

        lax.fori_loop(0, t >> 2, body, 0)
        done = t & ~3

        @pl.when((t & 2) == 2)
        def _():
            run_blocks([(done, False), (done + 1, False)], fixed_max=True)

        @pl.when(odd)
        def _():
            run_blocks([(t - 1, False)], fixed_max=True)

    @pl.when(jnp.logical_not(fixed_ok))
    def _():
        def body(i, carry):
            run_blocks([(2 * i, False), (2 * i + 1, False)])
            return carry

        lax.fori_loop(0, t >> 1, body, 0)

        @pl.when(odd)
        def _():
            run_blocks([(t - 1, False)])

    for h in range(ATT_HEADS):
        o_ref[h * HEAD_DIM:(h + 1) * HEAD_DIM, :] = (
            acc_ref[h, :HEAD_DIM, :] / acc_ref[h, HEAD_DIM:HEAD_DIM + 1, :]).astype(o_ref.dtype)


def _attn_prompt(u_t, k_bf, vt3, kmean, knmax, cos_t, sin_t, n, l):
    nq = MOBA_BLOCK
    nb = l // nq
    col = lambda b, t: b * nb + t
    return pl.pallas_call(
        _attn_prompt_kernel,
        grid=(n, nb),
        in_specs=[pl.BlockSpec((D_ATT, nq), lambda b, t: (ROW_Q // D_ATT, col(b, t))),
                  pl.BlockSpec((HEAD_DIM // 2, nq), lambda b, t: (0, t)),
                  pl.BlockSpec((HEAD_DIM // 2, nq), lambda b, t: (0, t)),
                  pl.BlockSpec((l, D_KV), lambda b, t: (b, 0)),
                  pl.BlockSpec((nb, D_KV, nq), lambda b, t: (b, 0, 0)),
                  pl.BlockSpec((None, KV_HEADS, nb, HEAD_DIM), lambda b, t: (b, 0, 0, 0)),
                  pl.BlockSpec((None, nb, LANES), lambda b, t: (b, 0, 0))],
        out_specs=pl.BlockSpec((D_ATT, nq), lambda b, t: (0, col(b, t))),
        out_shape=jax.ShapeDtypeStruct((D_ATT, n * l), BF16),
        scratch_shapes=[pltpu.VMEM((D_KV, ATT_HEADS * nq), F32),
                        pltpu.VMEM((D_KV, ATT_HEADS * nq), BF16),
                        pltpu.VMEM((nb, ATT_HEADS * nq), F32),
                        pltpu.VMEM((1, ATT_HEADS * nq), F32),
                        pltpu.VMEM((ATT_HEADS, HEAD_DIM + SUM_ROWS, nq), F32)],
        compiler_params=_cparams(("arbitrary", "arbitrary")),
        name="attn_prompt",
    )(u_t, cos_t, sin_t, k_bf, vt3, kmean, knmax)


def _merge_kernel(ya_ref, o_ref, zb_ref, ga_ref, gb_ref, x_ref, wa_ref, wb_ref, wo_ref, lng_ref, lnb_ref,
                  y_ref):
    br_a = jnp.dot(wa_ref[...], ya_ref[...], preferred_element_type=F32)
    ob = (o_ref[...].astype(F32) * _silu(zb_ref[...].astype(F32))).astype(BF16)
    br_b = jnp.dot(wb_ref[...], ob, preferred_element_type=F32)
    mixed = _sigmoid(ga_ref[...].astype(F32)) * br_a + _sigmoid(gb_ref[...].astype(F32)) * br_b
    out_t = jnp.dot(wo_ref[...], mixed.astype(BF16), preferred_element_type=F32)
    hsum = DEEPNORM_ALPHA * x_ref[...] + out_t.T
    mu = jnp.mean(hsum, axis=-1, keepdims=True)
    d = hsum - mu
    var = jnp.mean(d * d, axis=-1, keepdims=True)
    y_ref[...] = d * lax.rsqrt(var + NORM_EPS) * lng_ref[...] + lnb_ref[...]


def _merge(ya_t, o_t, g_t, x2d, wa_t, wb_t, wo_t, ln_g, ln_b, tm):
    t = x2d.shape[0]
    const = lambda i: (0, 0)
    wspec = pl.BlockSpec((D_MODEL, D_MODEL), const, pipeline_mode=pl.Buffered(1))
    return pl.pallas_call(
        _merge_kernel,
        grid=(t // tm,),
        in_specs=[pl.BlockSpec((D_SSM, tm), lambda i: (0, i)),
                  pl.BlockSpec((D_ATT, tm), lambda i: (0, i)),
                  pl.BlockSpec((D_ATT, tm), lambda i: (ROW_ZB // D_ATT, i)),
                  pl.BlockSpec((D_MODEL, tm), lambda i: (ROW_GA // D_MODEL, i)),
                  pl.BlockSpec((D_MODEL, tm), lambda i: (ROW_GB // D_MODEL, i)),
                  pl.BlockSpec((tm, D_MODEL), lambda i: (i, 0)),
                  wspec, wspec, wspec,
                  pl.BlockSpec((1, D_MODEL), const),
                  pl.BlockSpec((1, D_MODEL), const)],
        out_specs=pl.BlockSpec((tm, D_MODEL), lambda i: (i, 0)),
        out_shape=jax.ShapeDtypeStruct((t, D_MODEL), F32),
        compiler_params=_cparams(("arbitrary",)),
        name="merge",
    )(ya_t, o_t, g_t, g_t, g_t, x2d, wa_t, wb_t, wo_t, ln_g, ln_b)


SEQ_BLK = 8


def _split3(x):
    hi = x.astype(BF16)
    r1 = x - hi.astype(F32)
    mid = r1.astype(BF16)
    lo = (r1 - mid.astype(F32)).astype(BF16)
    return hi, mid, lo


def _ssd_sample_kernel(xbc_ref, za_ref, dt_ref, cst_ref, cw_ref, dtb_ref, alog_ref, dsk_ref, nw_ref, st_ref,
                       sto_ref, ya_ref,
                       dec_ref, xdt_ref, bm_ref, cmt_ref, xs_ref, y_ref):
    i = pl.program_id(0)
    ns = xbc_ref.shape[0]
    P = SSM_HEAD_DIM
    gs = D_SSM // SSM_GROUPS

    @pl.when(i == 0)
    def _():
        pre = xbc_ref[...]
        acc = pre * cw_ref[CONV_WIDTH - 1:CONV_WIDTH, :] + cw_ref[CONV_WIDTH:CONV_WIDTH + 1, :]
        for w in range(CONV_WIDTH - 1):
            acc = acc + cst_ref[w] * cw_ref[w:w + 1, :]
        xbc = _silu(acc)
        nbc = SSM_GROUPS * SSM_STATE
        xs_t = xbc[:, :D_SSM].T
        xs_ref[...] = xs_t
        bm_ref[...] = xbc[:, D_SSM:D_SSM + nbc]
        cmt_ref[...] = xbc[:, D_SSM + nbc:].T
        dt_t = _softplus(dt_ref[...] + dtb_ref[...])
        a_col = -jnp.exp(alog_ref[...])
        dec_t = jnp.exp(dt_t * a_col)
        for h in range(SSM_HEADS):
            rows = slice(h * P, (h + 1) * P)
            dec_ref[rows, :] = jnp.broadcast_to(dec_t[h:h + 1, :], (P, ns))
            xdt_ref[rows, :] = xs_t[rows] * dt_t[h:h + 1, :]
        y_ref[...] = jnp.zeros_like(y_ref)

    kk = lax.broadcasted_iota(jnp.int32, (ns, SEQ_BLK * SSM_STATE), 0)
    cc = lax.broadcasted_iota(jnp.int32, (ns, SEQ_BLK * SSM_STATE), 1)
    sel = (kk == i * SEQ_BLK + (cc >> 7)).astype(BF16)

    def bsel(x):
        hi, mid, lo = _split3(x)
        d = lambda a: jnp.dot(a, sel, preferred_element_type=F32)
        return d(hi) + d(mid) + d(lo)

    dec_b = bsel(dec_ref[...])
    xdt_b = bsel(xdt_ref[...])
    lane = lax.broadcasted_iota(jnp.int32, (SSM_STATE, ns), 1)
    for r in range(SEQ_BLK):
        n = i * SEQ_BLK + r
        cs = slice(r * SSM_STATE, (r + 1) * SSM_STATE)
        brow = bm_ref[pl.ds(n, 1), :]
        onehot = lane == n
        for g in range(SSM_GROUPS):
            rows = slice(g * gs, (g + 1) * gs)
            bg = brow[:, g * SSM_STATE:(g + 1) * SSM_STATE]
            hn = dec_b[rows, cs] * st_ref[r, rows, :] + xdt_b[rows, cs] * bg
            sto_ref[r, rows, :] = hn
            cg = jnp.where(onehot, cmt_ref[g * SSM_STATE:(g + 1) * SSM_STATE, :], 0.0)
            y_ref[rows, :] += jnp.dot(hn.astype(BF16), cg.astype(BF16), preferred_element_type=F32)

    @pl.when(i == pl.num_programs(0) - 1)
    def _():
        y_t = y_ref[...] + jnp.broadcast_to(dsk_ref[...], y_ref.shape) * xs_ref[...]
        w_b = jnp.broadcast_to(nw_ref[...], y_ref.shape)
        ya_ref[...] = _gated_rmsnorm_t(y_t, za_ref[...].astype(F32), w_b).astype(BF16)


def _ssd_sample(xbc_tok, u_t, g_t, cst, cw_rows, dtb_col, alog_col, dsk_col, nw_col, state):
    ns = u_t.shape[1]
    const2 = lambda i: (0, 0)
    return pl.pallas_call(
        _ssd_sample_kernel,
        grid=(ns // SEQ_BLK,),
        in_specs=[pl.BlockSpec((ns, CONV_DIM), const2),
                  pl.BlockSpec((D_SSM, ns), lambda i: (ROW_ZA // D_SSM, 0)),
                  pl.BlockSpec((LANES, ns), lambda i: (ROW_DT // LANES, 0)),
                  pl.BlockSpec((CONV_WIDTH - 1, ns, CONV_DIM), lambda i: (0, 0, 0)),
                  pl.BlockSpec((CARRY_ROWS, CONV_DIM), const2),
                  pl.BlockSpec((LANES, 1), const2),
                  pl.BlockSpec((LANES, 1), const2),
                  pl.BlockSpec((D_SSM, 1), const2),
                  pl.BlockSpec((D_SSM, 1), const2),
                  pl.BlockSpec((SEQ_BLK, D_SSM, SSM_STATE), lambda i: (i, 0, 0))],
        out_specs=[pl.BlockSpec((SEQ_BLK, D_SSM, SSM_STATE), lambda i: (i, 0, 0)),
                   pl.BlockSpec((D_SSM, ns), const2)],
        out_shape=[jax.ShapeDtypeStruct(state.shape, F32),
                   jax.ShapeDtypeStruct((D_SSM, ns), BF16)],
        scratch_shapes=[pltpu.VMEM((D_SSM, ns), F32),
                        pltpu.VMEM((D_SSM, ns), F32),
                        pltpu.VMEM((ns, SSM_GROUPS * SSM_STATE), F32),
                        pltpu.VMEM((SSM_GROUPS * SSM_STATE, ns), F32),
                        pltpu.VMEM((D_SSM, ns), F32),
                        pltpu.VMEM((D_SSM, ns), F32)],
        compiler_params=_cparams(("arbitrary",)),
        name="ssd_sample",
    )(xbc_tok, g_t, u_t, cst, cw_rows, dtb_col, alog_col, dsk_col, nw_col, state)


def _qprep_sample_kernel(q_ref, cos_ref, sin_ref, qt_ref, qf_ref):
    ns = q_ref.shape[1]
    qf_ref[...] = jnp.zeros_like(qf_ref)
    _rope_heads_t(q_ref, cos_ref[...], sin_ref[...], qf_ref, ATT_SCALE)
    for h in range(ATT_HEADS):
        qt_ref[h] = qf_ref[:, h * ns:(h + 1) * ns].T


def _qprep_sample(u_t, cos_t, sin_t):
    ns = u_t.shape[1]
    return pl.pallas_call(
        _qprep_sample_kernel,
        grid=(1,),
        in_specs=[pl.BlockSpec((D_ATT, ns), lambda i: (ROW_Q // D_ATT, 0)),
                  pl.BlockSpec((HEAD_DIM // 2, ns), lambda i: (0, 0)),
                  pl.BlockSpec((HEAD_DIM // 2, ns), lambda i: (0, 0))],
        out_specs=pl.BlockSpec((ATT_HEADS, ns, D_KV), lambda i: (0, 0, 0)),
        out_shape=jax.ShapeDtypeStruct((ATT_HEADS, ns, D_KV), F32),
        scratch_shapes=[pltpu.VMEM((D_KV, ATT_HEADS * ns), F32)],
        compiler_params=_cparams(("arbitrary",)),
        name="qprep_sample",
    )(u_t, cos_t, sin_t)


def _attn_sample_kernel(pt_ref, q_ref, kn_ref, vn_ref, *refs, n_pages, page):
    sb = q_ref.shape[0]
    k_refs = refs[:sb * n_pages]
    v_refs = refs[sb * n_pages:2 * sb * n_pages]
    o_ref, s_ref = refs[2 * sb * n_pages:]
    nb = n_pages * page // MOBA_BLOCK
    nkeys = n_pages * page
    nh = ATT_HEADS

    s_own = []
    for i in range(sb):
        qf = q_ref[i]
        qb = qf.astype(BF16)
        for j in range(n_pages):
            kp = k_refs[i * n_pages + j][...]
            s_ref[i * nh:(i + 1) * nh, j * page:(j + 1) * page] = jnp.dot(
                qb, kp.astype(BF16), preferred_element_type=F32)
        s_own.append(jnp.sum(qf * kn_ref[i], axis=1, keepdims=True))
    s_own = jnp.concatenate(s_own, axis=0)
    lane = lax.broadcasted_iota(jnp.int32, (sb * nh, LANES), 1)
    gate = jnp.zeros((sb * nh, LANES), F32)
    for b in range(nb):
        bsum = jnp.sum(s_ref[:, b * MOBA_BLOCK:(b + 1) * MOBA_BLOCK], axis=1, keepdims=True)
        gate = jnp.where(lane == b, bsum, gate)
    bias = _top3_mask(gate, float(nb), 1)
    expand = (lax.broadcasted_iota(jnp.int32, (LANES, nkeys), 0)
              == (lax.broadcasted_iota(jnp.int32, (LANES, nkeys), 1) >> int(math.log2(MOBA_BLOCK)))).astype(F32)
    sel_keys = jnp.dot(jnp.where(bias == 0.0, 1.0, 0.0), expand, preferred_element_type=F32)
    s = jnp.where(sel_keys > 0.5, s_ref[...], NEG)
    m = jnp.maximum(jnp.max(s, axis=1, keepdims=True), s_own)
    p = jnp.exp(s - m)
    p_own = jnp.exp(s_own - m)
    l = jnp.sum(p, axis=1, keepdims=True) + p_own
    pb = p.astype(BF16)
    nt = (((1,), (1,)), ((), ()))
    hh = lax.broadcasted_iota(jnp.int32, (nh, D_KV), 0) >> int(math.log2(Q_PER_KV))
    cg = lax.broadcasted_iota(jnp.int32, (nh, D_KV), 1) >> int(math.log2(HEAD_DIM))
    for i in range(sb):
        rows = slice(i * nh, (i + 1) * nh)
        acc = p_own[rows] * vn_ref[i]
        for j in range(n_pages):
            acc = acc + lax.dot_general(pb[rows, j * page:(j + 1) * page],
                                        v_refs[i * n_pages + j][...].astype(BF16), nt,
                                        preferred_element_type=F32)
        o_ref[i] = jnp.where(hh == cg, acc / l[rows], 0.0)


ATT_SEQ_BLK = 4


def _attn_sample(page_table_flat, q_rows, k_new, v_new, ck_t, cv_t, n_pages):
    ns = q_rows.shape[0]
    page = ck_t.shape[2]
    sb = ATT_SEQ_BLK

    def pspec(i, j):
        return pl.BlockSpec((None, D_KV, page), lambda n, pt, i=i, j=j: (pt[(n * sb + i) * n_pages + j], 0, 0))

    pages = [pspec(i, j) for i in range(sb) for j in range(n_pages)]
    row3 = lambda n, pt: (n, 0, 0)
    grid_spec = pltpu.PrefetchScalarGridSpec(
        num_scalar_prefetch=1,
        grid=(ns // sb,),
        in_specs=[pl.BlockSpec((sb, ATT_HEADS, D_KV), row3),
                  pl.BlockSpec((sb, 1, D_KV), row3),
                  pl.BlockSpec((sb, 1, D_KV), row3)] + pages + pages,
        out_specs=pl.BlockSpec((sb, ATT_HEADS, D_KV), row3),
        scratch_shapes=[pltpu.VMEM((sb * ATT_HEADS, n_pages * page), F32)],
    )
    return pl.pallas_call(
        functools.partial(_attn_sample_kernel, n_pages=n_pages, page=page),
        grid_spec=grid_spec,
        out_shape=jax.ShapeDtypeStruct((ns, ATT_HEADS, D_KV), F32),
        compiler_params=_cparams(("arbitrary",)),
        name="attn_sample",
    )(page_table_flat, q_rows, k_new, v_new, *([ck_t] * (sb * n_pages)), *([cv_t] * (sb * n_pages)))


def _ocompact_kernel(o_ref, ot_ref):
    for h in range(ATT_HEADS):
        g = h // Q_PER_KV
        ot_ref[h * HEAD_DIM:(h + 1) * HEAD_DIM, :] = o_ref[h].T[g * HEAD_DIM:(g + 1) * HEAD_DIM, :]


def _ocompact(o_h):
    ns = o_h.shape[1]
    return pl.pallas_call(
        _ocompact_kernel,
        grid=(1,),
        in_specs=[pl.BlockSpec((ATT_HEADS, ns, D_KV), lambda i: (0, 0, 0))],
        out_specs=pl.BlockSpec((D_ATT, ns), lambda i: (0, 0)),
        out_shape=jax.ShapeDtypeStruct((D_ATT, ns), F32),
        compiler_params=_cparams(("arbitrary",)),
        name="o_compact",
    )(o_h)


def _rope_tables(pos):
    half = HEAD_DIM // 2
    inv_freq = jnp.power(ROPE_THETA, -jnp.arange(half, dtype=F32) * 2.0 / HEAD_DIM)
    ang = pos.astype(F32)[:, None] * inv_freq[None, :]
    return jnp.cos(ang).T, jnp.sin(ang).T


def _prep_weights(w_in, conv_w, conv_b, dt_bias, a_log, d_skip, ssm_norm_w, w_a_out, w_b_out, w_out):
    sp = np.cumsum([D_SSM, D_SSM, SSM_GROUPS * SSM_STATE, SSM_GROUPS * SSM_STATE, SSM_HEADS,
                    D_ATT, D_KV, D_KV, D_ATT, D_MODEL])
    z_a, x_a, b_a, c_a, dt, q, k, v, z_b, g_a, g_b = jnp.split(w_in, [int(s) for s in sp], axis=1)
    u_cols = jnp.concatenate([q, dt], axis=1)
    u_cols = jnp.pad(u_cols, ((0, 0), (0, UT_ROWS - u_cols.shape[1])))
    w_x = jnp.concatenate([x_a, b_a, c_a], axis=1).astype(BF16)
    g_cols = jnp.concatenate([z_a, z_b, g_a, g_b], axis=1)
    w_t = jnp.concatenate([jnp.concatenate([u_cols[:, j * UT_TN:(j + 1) * UT_TN],
                                            g_cols[:, j * GT_TN:(j + 1) * GT_TN]], axis=1)
                           for j in range(PROJ_STEPS)], axis=1).T.astype(BF16)
    w_kv = jnp.concatenate([k, v], axis=1).T.astype(BF16)
    cw_rows = jnp.pad(jnp.concatenate([conv_w, conv_b[None, :]], axis=0),
                      ((0, CARRY_ROWS - CONV_WIDTH - 1), (0, 0)))
    pad = LANES - SSM_HEADS
    dtb_col = jnp.pad(dt_bias, (0, pad))[:, None]
    alog_col = jnp.pad(a_log, (0, pad))[:, None]
    dsk_col = jnp.repeat(d_skip, SSM_HEAD_DIM)[:, None]
    nw_col = ssm_norm_w[:, None]
    return (w_t, w_x, w_kv, cw_rows, dtb_col, alog_col, dsk_col, nw_col,
            w_a_out.T.astype(BF16), w_b_out.T.astype(BF16), w_out.T.astype(BF16))


def kernel(x_prompt, x_sample, cache_k, cache_v, state_conv, state_ssm, page_table,
           w_in, conv_w, conv_b, dt_bias, a_log, d_skip, ssm_norm_w, w_a_out, w_b_out, w_out, ln_g, ln_b):
    assert w_in.shape[0] == DEPTH
    n_p, l_p, _ = x_prompt.shape
    n_s, l_s, _ = x_sample.shape
    n_pages, page = page_table.shape[1], cache_k.shape[2]
    past_len = n_pages * page
    assert l_s == 1 and l_p % MOBA_BLOCK == 0 and past_len % MOBA_BLOCK == 0 and MOBA_BLOCK % page == 0
    assert n_s % LANES == 0 and past_len // MOBA_BLOCK >= MOBA_TOPK

    (w_t, w_x, w_kv, cw_rows, dtb_col, alog_col, dsk_col, nw_col, wa_t, wb_t, wo_t) = _prep_weights(
        w_in[0], conv_w[0], conv_b[0], dt_bias[0], a_log[0], d_skip[0], ssm_norm_w[0],
        w_a_out[0], w_b_out[0], w_out[0])
    lng, lnb = ln_g[0][None, :], ln_b[0][None, :]

    xp = x_prompt.reshape(n_p * l_p, D_MODEL)
    cos_t, sin_t = _rope_tables(jnp.arange(l_p, dtype=jnp.int32))
    u_t, g_t, xbc_p, kp_t, vp_t, vt3 = _in_proj(xp, w_t, w_x, w_kv, 1024 if l_p % 1024 == 0 else MOBA_BLOCK,
                                                n_p, l_p)
    rope_tile = 4 * MOBA_BLOCK if l_p % (4 * MOBA_BLOCK) == 0 else MOBA_BLOCK
    kp_rt, k_bf, kmean, knmax = _rope_k(kp_t, cos_t, sin_t, rope_tile, MOBA_BLOCK, BF16)
    ya_t, ssm_p = _ssd_prompt(xbc_p, u_t, g_t, n_p, l_p, cw_rows, dtb_col, alog_col, dsk_col, nw_col)
    nb = l_p // MOBA_BLOCK
    kmean_g = jnp.transpose(kmean.reshape(n_p, nb, KV_HEADS, HEAD_DIM), (0, 2, 1, 3))
    o_t = _attn_prompt(u_t, k_bf, vt3, kmean_g, knmax.reshape(n_p, nb, LANES), cos_t, sin_t, n_p, l_p)
    to_cache = lambda a: jnp.transpose(a.reshape(a.shape[0], KV_HEADS, HEAD_DIM, a.shape[2]), (0, 3, 1, 2))[None]
    tm_merge = 2 * MOBA_BLOCK if (n_p * l_p) % (2 * MOBA_BLOCK) == 0 else MOBA_BLOCK
    y_p = _merge(ya_t, o_t, g_t, xp, wa_t, wb_t, wo_t, lng, lnb, tm_merge)
    conv_p = jnp.stack([xbc_p[(b + 1) * l_p - (CONV_WIDTH - 1):(b + 1) * l_p] for b in range(n_p)])

    xs = x_sample.reshape(n_s, D_MODEL)
    pos_s = jnp.full((n_s,), past_len, dtype=jnp.int32)
    cos_s, sin_s = _rope_tables(pos_s)
    us_t, gs_t, xbc_s, ks_t, vs_t, _ = _in_proj(xs, w_t, w_x, w_kv, n_s, 1, n_s)
    k_s = _rope_k(ks_t, cos_s, sin_s, n_s, n_s, F32)[1]
    v_s = vs_t[0].T
    cst = jnp.transpose(state_conv[0], (1, 0, 2))
    st_in = state_ssm[0].reshape(n_s, D_SSM, SSM_STATE)
    ssm_s, yas_t = _ssd_sample(xbc_s, us_t, gs_t, cst, cw_rows, dtb_col, alog_col, dsk_col, nw_col, st_in)
    q_rows = jnp.transpose(_qprep_sample(us_t, cos_s, sin_s), (1, 0, 2))
    pages_t = lambda c: jnp.transpose(c[0], (0, 2, 3, 1)).reshape(c.shape[1], D_KV, page)
    o_s = _attn_sample(page_table.reshape(-1), q_rows, k_s[:, None, :], v_s[:, None, :],
                       pages_t(cache_k), pages_t(cache_v), n_pages)
    os_t = _ocompact(jnp.transpose(o_s, (1, 0, 2)))
    y_s = _merge(yas_t, os_t, gs_t, xs, wa_t, wb_t, wo_t, lng, lnb, n_s)
    conv_s = jnp.concatenate([state_conv[0][:, 1:], xbc_s[:, None, :]], axis=1)

    return (y_p.reshape(n_p, l_p, D_MODEL), y_s.reshape(n_s, l_s, D_MODEL),
            to_cache(kp_rt), to_cache(vp_t),
            k_s.reshape(1, n_s, l_s, KV_HEADS, HEAD_DIM), v_s.reshape(1, n_s, l_s, KV_HEADS, HEAD_DIM),
            conv_p[None], conv_s[None],
            ssm_p.reshape(1, n_p, SSM_HEADS, SSM_HEAD_DIM, SSM_STATE),
            ssm_s.reshape(1, n_s, SSM_HEADS, SSM_HEAD_DIM, SSM_STATE))
```

```python
import functools
import math

import numpy as np
import jax
import jax.numpy as jnp
from jax import lax
from jax.experimental import pallas as pl
from jax.experimental.pallas import tpu as pltpu

F32 = jnp.float32
BF16 = jnp.bfloat16

D_MODEL = 1024
HEAD_DIM = 64
SSM_HEADS = 16
SSM_HEAD_DIM = 64
SSM_GROUPS = 4
SSM_STATE = 128
CONV_WIDTH = 4
D_SSM = SSM_HEADS * SSM_HEAD_DIM
CONV_DIM = D_SSM + 2 * SSM_GROUPS * SSM_STATE
SSD_CHUNK = 128
ATT_HEADS = 16
KV_HEADS = 4
Q_PER_KV = ATT_HEADS // KV_HEADS
D_ATT = ATT_HEADS * HEAD_DIM
D_KV = KV_HEADS * HEAD_DIM
MOBA_BLOCK = 256
MOBA_TOPK = 3
ATT_SCALE = HEAD_DIM ** -0.5
ROPE_THETA = 10000.0
NORM_EPS = 1e-5
DEPTH = 1
DEEPNORM_ALPHA = (2 * DEPTH) ** 0.25

LANES = 128
VMEM_LIMIT = 56 * 1024 * 1024

ROW_Q = 0
ROW_DT = ROW_Q + D_ATT
UT_ROWS = ROW_DT + LANES
ROW_ZA = 0
ROW_ZB = ROW_ZA + D_SSM
ROW_GA = ROW_ZB + D_ATT
ROW_GB = ROW_GA + D_MODEL
GT_ROWS = ROW_GB + D_MODEL
PROJ_STEPS = 4
UT_TN = UT_ROWS // PROJ_STEPS
GT_TN = GT_ROWS // PROJ_STEPS
XT_TN = CONV_DIM // PROJ_STEPS

NEG = -1e30
HI = lax.Precision.HIGHEST


def _sigmoid(x):
    return 0.5 * jnp.tanh(0.5 * x) + 0.5


def _silu(x):
    return x * _sigmoid(x)


def _softplus(x):
    return jnp.maximum(x, 0.0) + jnp.log(1.0 + jnp.exp(-jnp.abs(x)))


def _cparams(sem):
    return pltpu.CompilerParams(dimension_semantics=sem, vmem_limit_bytes=VMEM_LIMIT)


def _inproj_kernel(x_ref, wt_ref, wx_ref, wkv_ref, ut_ref, gt_ref, xt_ref, k_ref, v_ref, vb_ref):
    x = x_ref[...].astype(BF16)
    nt = (((1,), (1,)), ((), ()))
    res = lax.dot_general(wt_ref[...], x, nt, preferred_element_type=F32)
    ut_ref[...] = res[:UT_TN]
    gt_ref[...] = res[UT_TN:].astype(BF16)
    xt_ref[...] = jnp.dot(x, wx_ref[...], preferred_element_type=F32)

    @pl.when(pl.program_id(1) == 0)
    def _():
        kv = lax.dot_general(wkv_ref[...], x, nt, preferred_element_type=F32)
        k_ref[...] = kv[:D_KV]
        v_ref[...] = kv[D_KV:]
        wb = vb_ref.shape[2]
        for c in range(vb_ref.shape[0]):
            vb_ref[c] = kv[D_KV:, c * wb:(c + 1) * wb].astype(BF16)


def _in_proj(x2d, w_t, w_x, w_kv, tm, n, l):
    t = x2d.shape[0]
    per = l // tm
    wb = min(tm, MOBA_BLOCK)
    kv_spec = pl.BlockSpec((None, D_KV, tm), lambda i, j: (i // per, 0, i % per))
    return pl.pallas_call(
        _inproj_kernel,
        grid=(t // tm, PROJ_STEPS),
        in_specs=[pl.BlockSpec((tm, D_MODEL), lambda i, j: (i, 0)),
                  pl.BlockSpec((UT_TN + GT_TN, D_MODEL), lambda i, j: (j, 0)),
                  pl.BlockSpec((D_MODEL, XT_TN), lambda i, j: (0, j)),
                  pl.BlockSpec((2 * D_KV, D_MODEL), lambda i, j: (0, 0))],
        out_specs=[pl.BlockSpec((UT_TN, tm), lambda i, j: (j, i)),
                   pl.BlockSpec((GT_TN, tm), lambda i, j: (j, i)),
                   pl.BlockSpec((tm, XT_TN), lambda i, j: (i, j)), kv_spec, kv_spec,
                   pl.BlockSpec((tm // wb, D_KV, wb), lambda i, j: (i, 0, 0))],
        out_shape=[jax.ShapeDtypeStruct((UT_ROWS, t), F32),
                   jax.ShapeDtypeStruct((GT_ROWS, t), BF16),
                   jax.ShapeDtypeStruct((t, CONV_DIM), F32),
                   jax.ShapeDtypeStruct((n, D_KV, l), F32),
                   jax.ShapeDtypeStruct((n, D_KV, l), F32),
                   jax.ShapeDtypeStruct((t // wb, D_KV, wb), BF16)],
        compiler_params=_cparams(("arbitrary", "arbitrary")),
        name="in_proj",
    )(x2d, w_t, w_x, w_kv)


def _ropek_kernel(k_ref, cos_ref, sin_ref, ko_ref, kt_ref, km_ref, kn_ref):
    half = HEAD_DIM // 2
    nblk = km_ref.shape[0]
    blk = k_ref.shape[1] // nblk
    grp = (lax.broadcasted_iota(jnp.int32, (D_KV, LANES), 0) >> int(math.log2(HEAD_DIM))
           == lax.broadcasted_iota(jnp.int32, (D_KV, LANES), 1)).astype(F32)
    for i in range(nblk):
        sl = slice(i * blk, (i + 1) * blk)
        cos, sin = cos_ref[:, sl], sin_ref[:, sl]
        rows = []
        for g in range(KV_HEADS):
            x1 = k_ref[g * HEAD_DIM:g * HEAD_DIM + half, sl]
            x2 = k_ref[g * HEAD_DIM + half:(g + 1) * HEAD_DIM, sl]
            rows += [x1 * cos - x2 * sin, x2 * cos + x1 * sin]
        kr = jnp.concatenate(rows, axis=0)
        ko_ref[:, sl] = kr
        kt = kr.T
        kq = kt.astype(kt_ref.dtype)
        kt_ref[sl, :] = kq
        km_ref[i] = jnp.mean(kt, axis=0, keepdims=True)
        kf = kq.astype(F32)
        nrm2 = jnp.dot(kf * kf, grp, preferred_element_type=F32)
        kn_ref[i] = jnp.sqrt(jnp.max(nrm2, axis=0, keepdims=True)) * (1.0 + 2.0 ** -7)


def _rope_k(k_t, cos_t, sin_t, tile, blk, tok_dtype):
    n, _, l = k_t.shape
    nt = l // tile
    per = tile // blk
    return pl.pallas_call(
        _ropek_kernel,
        grid=(n, nt),
        in_specs=[pl.BlockSpec((None, D_KV, tile), lambda b, i: (b, 0, i)),
                  pl.BlockSpec((HEAD_DIM // 2, tile), lambda b, i: (0, i)),
                  pl.BlockSpec((HEAD_DIM // 2, tile), lambda b, i: (0, i))],
        out_specs=[pl.BlockSpec((None, D_KV, tile), lambda b, i: (b, 0, i)),
                   pl.BlockSpec((tile, D_KV), lambda b, i: (b * nt + i, 0)),
                   pl.BlockSpec((per, 1, D_KV), lambda b, i: (b * nt + i, 0, 0)),
                   pl.BlockSpec((per, 1, LANES), lambda b, i: (b * nt + i, 0, 0))],
        out_shape=[jax.ShapeDtypeStruct((n, D_KV, l), F32),
                   jax.ShapeDtypeStruct((n * l, D_KV), tok_dtype),
                   jax.ShapeDtypeStruct((n * nt * per, 1, D_KV), F32),
                   jax.ShapeDtypeStruct((n * nt * per, 1, LANES), F32)],
        compiler_params=_cparams(("arbitrary", "arbitrary")),
        name="rope_k",
    )(k_t, cos_t, sin_t)


def _gated_rmsnorm_t(y_t, z_t, w_b):
    g = y_t * _silu(z_t)
    gsz = D_SSM // SSM_GROUPS
    outs = []
    for i in range(SSM_GROUPS):
        gg = g[i * gsz:(i + 1) * gsz]
        ms = jnp.mean(gg * gg, axis=0, keepdims=True)
        outs.append(gg * lax.rsqrt(ms + NORM_EPS))
    return jnp.concatenate(outs, axis=0) * w_b


CARRY_ROWS = 8


def _ssd_prompt_kernel(xbc_ref, za_ref, dt_ref, cw_ref, dtb_ref, alog_ref, dsk_ref, nw_ref,
                       ya_ref, st_ref, stage_ref, pb_ref):
    c = pl.program_id(1)
    L = SSD_CHUNK

    @pl.when((pl.program_id(0) == 0) & (c == 0))
    def _():
        pb_ref[0] = jnp.broadcast_to(dsk_ref[...], pb_ref.shape[1:])
        pb_ref[1] = jnp.broadcast_to(nw_ref[...], pb_ref.shape[1:])

    @pl.when(c == 0)
    def _():
        stage_ref[:CARRY_ROWS, :] = jnp.zeros((CARRY_ROWS, CONV_DIM), F32)
        st_ref[...] = jnp.zeros_like(st_ref)

    for k in range(xbc_ref.shape[0] // L):
        _ssd_chunk(xbc_ref, za_ref, dt_ref, cw_ref, dtb_ref, alog_ref, ya_ref, st_ref, stage_ref, pb_ref,
                   slice(k * L, (k + 1) * L))


def _ssd_chunk(xbc_ref, za_ref, dt_ref, cw_ref, dtb_ref, alog_ref, ya_ref, st_ref, stage_ref, pb_ref, sl):
    L = SSD_CHUNK
    cur = xbc_ref[sl, :]
    stage_ref[CARRY_ROWS:, :] = cur
    acc = cur * cw_ref[CONV_WIDTH - 1:CONV_WIDTH, :] + cw_ref[CONV_WIDTH:CONV_WIDTH + 1, :]
    for s in range(1, CONV_WIDTH):
        acc = acc + stage_ref[CARRY_ROWS - s:CARRY_ROWS - s + L, :] * cw_ref[CONV_WIDTH - 1 - s:CONV_WIDTH - s, :]
    stage_ref[:CARRY_ROWS, :] = cur[L - CARRY_ROWS:, :]
    xbc = _silu(acc)
    nbc = SSM_GROUPS * SSM_STATE
    xs_t = xbc[:, :D_SSM].T
    bm = xbc[:, D_SSM:D_SSM + nbc]
    cm_t = xbc[:, D_SSM + nbc:].T

    dt_t = _softplus(dt_ref[:, sl] + dtb_ref[...])
    a_col = -jnp.exp(alog_ref[...]) * math.log2(math.e)
    adt_t = dt_t * a_col
    ii = lax.broadcasted_iota(jnp.int32, (L, L), 0)
    jj = lax.broadcasted_iota(jnp.int32, (L, L), 1)
    upper = (ii <= jj).astype(F32)
    lower = (jj <= ii).astype(F32)
    acs_t = jnp.dot(adt_t, upper, precision=HI, preferred_element_type=F32)
    acs = lax.dot_general(lower, adt_t, (((1,), (1,)), ((), ())), precision=HI,
                          preferred_element_type=F32)
    causal_t = ii <= jj

    dsk_b = pb_ref[0]
    ys = []
    R = SSM_HEADS // SSM_GROUPS
    P = SSM_HEAD_DIM
    for g in range(SSM_GROUPS):
        cmg_t = cm_t[g * SSM_STATE:(g + 1) * SSM_STATE]
        bmg = bm[:, g * SSM_STATE:(g + 1) * SSM_STATE].astype(BF16)
        cmg_tb = cmg_t.astype(BF16)
        cb_t = jnp.dot(bmg, cmg_tb, preferred_element_type=F32)
        for r in range(R):
            h = g * R + r
            row = acs_t[h:h + 1, :]
            col = acs[:, h:h + 1]
            dec = jnp.exp2(jnp.where(causal_t, row - col, NEG))
            m_t = (cb_t * dec).astype(BF16)
            xs_h = xs_t[h * P:(h + 1) * P]
            xdt_h = xs_h * dt_t[h:h + 1, :]
            y_diag = jnp.dot(xdt_h.astype(BF16), m_t, preferred_element_type=F32)
            hprev = st_ref[0, h * P:(h + 1) * P, :]
            y_off = jnp.dot(hprev.astype(BF16), cmg_tb, preferred_element_type=F32) * jnp.exp2(row)
            last = acs_t[h:h + 1, L - 1:L]
            te = jnp.exp2(last - row)
            st = jnp.dot((xdt_h * te).astype(BF16), bmg, preferred_element_type=F32)
            st_ref[0, h * P:(h + 1) * P, :] = jnp.exp2(last) * hprev + st
            ys.append(y_diag + y_off + dsk_b[h * P:(h + 1) * P] * xs_h)
    y_t = jnp.concatenate(ys, axis=0)
    ya_ref[:, sl] = _gated_rmsnorm_t(y_t, za_ref[:, sl].astype(F32), pb_ref[1]).astype(BF16)


SSD_CHUNKS_PER_STEP = 2


def _ssd_prompt(xbc_tok, u_t, g_t, n, l, cw_rows, dtb_col, alog_col, dsk_col, nw_col):
    tile = SSD_CHUNK * (SSD_CHUNKS_PER_STEP if l % (SSD_CHUNK * SSD_CHUNKS_PER_STEP) == 0 else 1)
    nc = l // tile
    col = lambda b, c: b * nc + c
    const = lambda b, c: (0, 0)
    return pl.pallas_call(
        _ssd_prompt_kernel,
        grid=(n, nc),
        in_specs=[pl.BlockSpec((tile, CONV_DIM), lambda b, c: (col(b, c), 0)),
                  pl.BlockSpec((D_SSM, tile), lambda b, c: (ROW_ZA // D_SSM, col(b, c))),
                  pl.BlockSpec((LANES, tile), lambda b, c: (ROW_DT // LANES, col(b, c))),
                  pl.BlockSpec((CARRY_ROWS, CONV_DIM), const),
                  pl.BlockSpec((LANES, 1), const),
                  pl.BlockSpec((LANES, 1), const),
                  pl.BlockSpec((D_SSM, 1), const),
                  pl.BlockSpec((D_SSM, 1), const)],
        out_specs=[pl.BlockSpec((D_SSM, tile), lambda b, c: (0, col(b, c))),
                   pl.BlockSpec((1, D_SSM, SSM_STATE), lambda b, c: (b, 0, 0))],
        out_shape=[jax.ShapeDtypeStruct((D_SSM, n * l), BF16),
                   jax.ShapeDtypeStruct((n, D_SSM, SSM_STATE), F32)],
        scratch_shapes=[pltpu.VMEM((CARRY_ROWS + SSD_CHUNK, CONV_DIM), F32),
                        pltpu.VMEM((2, D_SSM, LANES), F32)],
        compiler_params=_cparams(("arbitrary", "arbitrary")),
        name="ssd_prompt",
    )(xbc_tok, g_t, u_t, cw_rows, dtb_col, alog_col, dsk_col, nw_col)


def _rope_heads_t(q_ref, cos, sin, dst_ref, scale):
    nq = q_ref.shape[1]
    half = HEAD_DIM // 2
    for h in range(ATT_HEADS):
        x1 = q_ref[h * HEAD_DIM:h * HEAD_DIM + half, :]
        x2 = q_ref[h * HEAD_DIM + half:(h + 1) * HEAD_DIM, :]
        g = h // Q_PER_KV
        dst_ref[g * HEAD_DIM:g * HEAD_DIM + half, h * nq:(h + 1) * nq] = (x1 * cos - x2 * sin) * scale
        dst_ref[g * HEAD_DIM + half:(g + 1) * HEAD_DIM, h * nq:(h + 1) * nq] = (x2 * cos + x1 * sin) * scale


def _top3_mask(gate, nvalid, axis):
    nb = gate.shape[axis]
    bi = lax.broadcasted_iota(jnp.int32, gate.shape, axis).astype(F32)
    valid = bi < nvalid
    gate = jnp.where(valid, gate, -jnp.inf)
    sel = jnp.zeros(gate.shape, F32)
    for _ in range(MOBA_TOPK):
        mx = jnp.max(gate, axis=axis, keepdims=True)
        idx = jnp.min(jnp.where(gate == mx, bi, float(nb)), axis=axis, keepdims=True)
        pick = bi == idx
        sel = jnp.where(pick, 1.0, sel)
        gate = jnp.where(pick, -jnp.inf, gate)
    return jnp.where((sel > 0.0) & valid, 0.0, NEG)


SUM_ROWS = 16
MAX_EXCESS = 60.0
SCORES_AHEAD = 4


def _attn_prompt_kernel(q_ref, cos_ref, sin_ref, k_ref, vt_ref, km_ref, kn_ref, o_ref,
                        qf_ref, qb_ref, bias_ref, m_ref, acc_ref):
    t = pl.program_id(1)
    nq = MOBA_BLOCK

    @pl.when((pl.program_id(0) == 0) & (t == 0))
    def _():
        qf_ref[...] = jnp.zeros_like(qf_ref)

    _rope_heads_t(q_ref, cos_ref[...], sin_ref[...], qf_ref, ATT_SCALE * math.log2(math.e))
    qb_ref[...] = qf_ref[...].astype(BF16)
    ones = jnp.ones((SUM_ROWS, nq), BF16)
    gw = Q_PER_KV * nq
    qn = []
    for g in range(KV_HEADS):
        qg = qf_ref[g * HEAD_DIM:(g + 1) * HEAD_DIM, g * gw:(g + 1) * gw]
        gate = jnp.dot(km_ref[g], qg, precision=HI, preferred_element_type=F32)
        bias_ref[:, g * gw:(g + 1) * gw] = _top3_mask(gate, t.astype(F32), 0)
        qn.append(jnp.sqrt(jnp.sum(qg * qg, axis=0, keepdims=True)) * (1.0 + 2.0 ** -7))
    causal = lax.broadcasted_iota(jnp.int32, (nq, nq), 0) <= lax.broadcasted_iota(jnp.int32, (nq, nq), 1)

    def run_blocks(blocks, fixed_max=False):
        loaded = []
        for bk, own in blocks:
            kb = k_ref[pl.ds(pl.multiple_of(bk * nq, nq), nq), :]
            brow = None if own else bias_ref[pl.ds(bk, 1), :]
            loaded.append((kb, vt_ref[bk], brow, own))
        items = [(j, h) for j in range(len(blocks)) for h in range(ATT_HEADS)]
        qk = lambda j, h: jnp.dot(loaded[j][0], qb_ref[:, h * nq:(h + 1) * nq], preferred_element_type=F32)
        pending = [qk(*it) for it in items[:SCORES_AHEAD]]
        for idx, (j, h) in enumerate(items):
            s = pending.pop(0)
            if idx + SCORES_AHEAD < len(items):
                pending.append(qk(*items[idx + SCORES_AHEAD]))
            _, vtb, brow, own = loaded[j]
            g = h // Q_PER_KV
            cols = slice(h * nq, (h + 1) * nq)
            if own:
                s = jnp.where(causal, s, NEG)
                m_new = jnp.max(s, axis=0, keepdims=True)
                sub = m_new
            elif fixed_max:
                m_new = None
                sub = jnp.where(brow[:, cols] == 0.0, m_ref[:, cols], -NEG)
            else:
                bh = brow[:, cols]
                m_old = m_ref[:, cols]
                m_new = jnp.maximum(m_old, jnp.max(s, axis=0, keepdims=True) + bh)
                sub = jnp.where(bh == 0.0, m_new, -NEG)
            p = jnp.exp2(s - sub).astype(BF16)
            va = jnp.concatenate([vtb[g * HEAD_DIM:(g + 1) * HEAD_DIM, :], ones], axis=0)
            pv = jnp.dot(va, p, preferred_element_type=F32)
            if own:
                acc_ref[h] = pv
                m_ref[:, cols] = m_new
            elif fixed_max:
                acc_ref[h] += pv
            else:
                acc_ref[h] = acc_ref[h] * jnp.exp2(m_old - m_new) + pv
                m_ref[:, cols] = m_new

    run_blocks([(t, True)])

    bi = lax.broadcasted_iota(jnp.int32, kn_ref.shape, 0)
    kmax = jnp.max(jnp.where(bi < t, kn_ref[...], 0.0), axis=0, keepdims=True)
    excess = [jnp.max(qn[g] * kmax[:, g:g + 1] - m_ref[:, g * gw:(g + 1) * gw]) for g in range(KV_HEADS)]
    fixed_ok = functools.reduce(jnp.maximum, excess) <= MAX_EXCESS
    odd = (t & 1) == 1

    @pl.when(fixed_ok)
    def _():
        def body(i, carry):
            run_blocks([(8 * i + j, False) for j in range(8)], fixed_max=True)
            return carry

        lax.fori_loop(0, t >> 3, body, 0)
        done8 = t & ~7
        done = t & ~3

        @pl.when((t & 4) == 4)
        def _():
            run_blocks([(done8 + j, False) for j in range(4)], fixed_max=True)

        @pl.when((t & 2) == 2)
        def _():
            run_blocks([(done, False), (done + 1, False)], fixed_max=True)

        @pl.when(odd)
        def _():
            run_blocks([(t - 1, False)], fixed_max=True)

    @pl.when(jnp.logical_not(fixed_ok))
    def _():
        def body(i, carry):
            run_blocks([(2 * i, False), (2 * i + 1, False)])
            return carry

        lax.fori_loop(0, t >> 1, body, 0)

        @pl.when(odd)
        def _():
            run_blocks([(t - 1, False)])

    for h in range(ATT_HEADS):
        o_ref[h * HEAD_DIM:(h + 1) * HEAD_DIM, :] = (
            acc_ref[h, :HEAD_DIM, :] / acc_ref[h, HEAD_DIM:HEAD_DIM + 1, :]).astype(o_ref.dtype)


def _attn_prompt(u_t, k_bf, vt3, kmean, knmax, cos_t, sin_t, n, l):
    nq = MOBA_BLOCK
    nb = l // nq
    col = lambda b, t: b * nb + t
    return pl.pallas_call(
        _attn_prompt_kernel,
        grid=(n, nb),
        in_specs=[pl.BlockSpec((D_ATT, nq), lambda b, t: (ROW_Q // D_ATT, col(b, t))),
                  pl.BlockSpec((HEAD_DIM // 2, nq), lambda b, t: (0, t)),
                  pl.BlockSpec((HEAD_DIM // 2, nq), lambda b, t: (0, t)),
                  pl.BlockSpec((l, D_KV), lambda b, t: (b, 0)),
                  pl.BlockSpec((nb, D_KV, nq), lambda b, t: (b, 0, 0)),
                  pl.BlockSpec((None, KV_HEADS, nb, HEAD_DIM), lambda b, t: (b, 0, 0, 0)),
                  pl.BlockSpec((None, nb, LANES), lambda b, t: (b, 0, 0))],
        out_specs=pl.BlockSpec((D_ATT, nq), lambda b, t: (0, col(b, t))),
        out_shape=jax.ShapeDtypeStruct((D_ATT, n * l), BF16),
        scratch_shapes=[pltpu.VMEM((D_KV, ATT_HEADS * nq), F32),
                        pltpu.VMEM((D_KV, ATT_HEADS * nq), BF16),
                        pltpu.VMEM((nb, ATT_HEADS * nq), F32),
                        pltpu.VMEM((1, ATT_HEADS * nq), F32),
                        pltpu.VMEM((ATT_HEADS, HEAD_DIM + SUM_ROWS, nq), F32)],
        compiler_params=_cparams(("arbitrary", "arbitrary")),
        name="attn_prompt",
    )(u_t, cos_t, sin_t, k_bf, vt3, kmean, knmax)


def _merge_kernel(ya_ref, o_ref, zb_ref, ga_ref, gb_ref, x_ref, wa_ref, wb_ref, wo_ref, lng_ref, lnb_ref,
                  y_ref):
    br_a = jnp.dot(wa_ref[...], ya_ref[...], preferred_element_type=F32)
    ob = (o_ref[...].astype(F32) * _silu(zb_ref[...].astype(F32))).astype(BF16)
    br_b = jnp.dot(wb_ref[...], ob, preferred_element_type=F32)
    mixed = _sigmoid(ga_ref[...].astype(F32)) * br_a + _sigmoid(gb_ref[...].astype(F32)) * br_b
    out_t = jnp.dot(wo_ref[...], mixed.astype(BF16), preferred_element_type=F32)
    hsum = DEEPNORM_ALPHA * x_ref[...] + out_t.T
    mu = jnp.mean(hsum, axis=-1, keepdims=True)
    d = hsum - mu
    var = jnp.mean(d * d, axis=-1, keepdims=True)
    y_ref[...] = d * lax.rsqrt(var + NORM_EPS) * lng_ref[...] + lnb_ref[...]


def _merge(ya_t, o_t, g_t, x2d, wa_t, wb_t, wo_t, ln_g, ln_b, tm):
    t = x2d.shape[0]
    const = lambda i: (0, 0)
    wspec = pl.BlockSpec((D_MODEL, D_MODEL), const, pipeline_mode=pl.Buffered(1))
    return pl.pallas_call(
        _merge_kernel,
        grid=(t // tm,),
        in_specs=[pl.BlockSpec((D_SSM, tm), lambda i: (0, i)),
                  pl.BlockSpec((D_ATT, tm), lambda i: (0, i)),
                  pl.BlockSpec((D_ATT, tm), lambda i: (ROW_ZB // D_ATT, i)),
                  pl.BlockSpec((D_MODEL, tm), lambda i: (ROW_GA // D_MODEL, i)),
                  pl.BlockSpec((D_MODEL, tm), lambda i: (ROW_GB // D_MODEL, i)),
                  pl.BlockSpec((tm, D_MODEL), lambda i: (i, 0)),
                  wspec, wspec, wspec,
                  pl.BlockSpec((1, D_MODEL), const),
                  pl.BlockSpec((1, D_MODEL), const)],
        out_specs=pl.BlockSpec((tm, D_MODEL), lambda i: (i, 0)),
        out_shape=jax.ShapeDtypeStruct((t, D_MODEL), F32),
        compiler_params=_cparams(("arbitrary",)),
        name="merge",
    )(ya_t, o_t, g_t, g_t, g_t, x2d, wa_t, wb_t, wo_t, ln_g, ln_b)


SEQ_BLK = 8


def _split3(x):
    hi = x.astype(BF16)
    r1 = x - hi.astype(F32)
    mid = r1.astype(BF16)
    lo = (r1 - mid.astype(F32)).astype(BF16)
    return hi, mid, lo


def _ssd_sample_kernel(xbc_ref, za_ref, dt_ref, cst_ref, cw_ref, dtb_ref, alog_ref, dsk_ref, nw_ref, st_ref,
                       sto_ref, ya_ref,
                       dec_ref, xdt_ref, bm_ref, cmt_ref, xs_ref, y_ref):
    i = pl.program_id(0)
    ns = xbc_ref.shape[0]
    P = SSM_HEAD_DIM
    gs = D_SSM // SSM_GROUPS

    @pl.when(i == 0)
    def _():
        pre = xbc_ref[...]
        acc = pre * cw_ref[CONV_WIDTH - 1:CONV_WIDTH, :] + cw_ref[CONV_WIDTH:CONV_WIDTH + 1, :]
        for w in range(CONV_WIDTH - 1):
            acc = acc + cst_ref[w] * cw_ref[w:w + 1, :]
        xbc = _silu(acc)
        nbc = SSM_GROUPS * SSM_STATE
        xs_t = xbc[:, :D_SSM].T
        xs_ref[...] = xs_t
        bm_ref[...] = xbc[:, D_SSM:D_SSM + nbc]
        cmt_ref[...] = xbc[:, D_SSM + nbc:].T
        dt_t = _softplus(dt_ref[...] + dtb_ref[...])
        a_col = -jnp.exp(alog_ref[...])
        dec_t = jnp.exp(dt_t * a_col)
        for h in range(SSM_HEADS):
            rows = slice(h * P, (h + 1) * P)
            dec_ref[rows, :] = jnp.broadcast_to(dec_t[h:h + 1, :], (P, ns))
            xdt_ref[rows, :] = xs_t[rows] * dt_t[h:h + 1, :]
        y_ref[...] = jnp.zeros_like(y_ref)

    kk = lax.broadcasted_iota(jnp.int32, (ns, SEQ_BLK * SSM_STATE), 0)
    cc = lax.broadcasted_iota(jnp.int32, (ns, SEQ_BLK * SSM_STATE), 1)
    sel = (kk == i * SEQ_BLK + (cc >> 7)).astype(BF16)

    def bsel(x):
        hi, mid, lo = _split3(x)
        d = lambda a: jnp.dot(a, sel, preferred_element_type=F32)
        return d(hi) + d(mid) + d(lo)

    dec_b = bsel(dec_ref[...])
    xdt_b = bsel(xdt_ref[...])
    lane = lax.broadcasted_iota(jnp.int32, (SSM_STATE, ns), 1)
    for r in range(SEQ_BLK):
        n = i * SEQ_BLK + r
        cs = slice(r * SSM_STATE, (r + 1) * SSM_STATE)
        brow = bm_ref[pl.ds(n, 1), :]
        onehot = lane == n
        for g in range(SSM_GROUPS):
            rows = slice(g * gs, (g + 1) * gs)
            bg = brow[:, g * SSM_STATE:(g + 1) * SSM_STATE]
            hn = dec_b[rows, cs] * st_ref[r, rows, :] + xdt_b[rows, cs] * bg
            sto_ref[r, rows, :] = hn
            cg = jnp.where(onehot, cmt_ref[g * SSM_STATE:(g + 1) * SSM_STATE, :], 0.0)
            y_ref[rows, :] += jnp.dot(hn.astype(BF16), cg.astype(BF16), preferred_element_type=F32)

    @pl.when(i == pl.num_programs(0) - 1)
    def _():
        y_t = y_ref[...] + jnp.broadcast_to(dsk_ref[...], y_ref.shape) * xs_ref[...]
        w_b = jnp.broadcast_to(nw_ref[...], y_ref.shape)
        ya_ref[...] = _gated_rmsnorm_t(y_t, za_ref[...].astype(F32), w_b).astype(BF16)


def _ssd_sample(xbc_tok, u_t, g_t, cst, cw_rows, dtb_col, alog_col, dsk_col, nw_col, state):
    ns = u_t.shape[1]
    const2 = lambda i: (0, 0)
    return pl.pallas_call(
        _ssd_sample_kernel,
        grid=(ns // SEQ_BLK,),
        in_specs=[pl.BlockSpec((ns, CONV_DIM), const2),
                  pl.BlockSpec((D_SSM, ns), lambda i: (ROW_ZA // D_SSM, 0)),
                  pl.BlockSpec((LANES, ns), lambda i: (ROW_DT // LANES, 0)),
                  pl.BlockSpec((CONV_WIDTH - 1, ns, CONV_DIM), lambda i: (0, 0, 0)),
                  pl.BlockSpec((CARRY_ROWS, CONV_DIM), const2),
                  pl.BlockSpec((LANES, 1), const2),
                  pl.BlockSpec((LANES, 1), const2),
                  pl.BlockSpec((D_SSM, 1), const2),
                  pl.BlockSpec((D_SSM, 1), const2),
                  pl.BlockSpec((SEQ_BLK, D_SSM, SSM_STATE), lambda i: (i, 0, 0))],
        out_specs=[pl.BlockSpec((SEQ_BLK, D_SSM, SSM_STATE), lambda i: (i, 0, 0)),
                   pl.BlockSpec((D_SSM, ns), const2)],
        out_shape=[jax.ShapeDtypeStruct(state.shape, F32),
                   jax.ShapeDtypeStruct((D_SSM, ns), BF16)],
        scratch_shapes=[pltpu.VMEM((D_SSM, ns), F32),
                        pltpu.VMEM((D_SSM, ns), F32),
                        pltpu.VMEM((ns, SSM_GROUPS * SSM_STATE), F32),
                        pltpu.VMEM((SSM_GROUPS * SSM_STATE, ns), F32),
                        pltpu.VMEM((D_SSM, ns), F32),
                        pltpu.VMEM((D_SSM, ns), F32)],
        compiler_params=_cparams(("arbitrary",)),
        name="ssd_sample",
    )(xbc_tok, g_t, u_t, cst, cw_rows, dtb_col, alog_col, dsk_col, nw_col, state)


def _qprep_sample_kernel(q_ref, cos_ref, sin_ref, qt_ref, qf_ref):
    ns = q_ref.shape[1]
    qf_ref[...] = jnp.zeros_like(qf_ref)
    _rope_heads_t(q_ref, cos_ref[...], sin_ref[...], qf_ref, ATT_SCALE)
    for h in range(ATT_HEADS):
        qt_ref[h] = qf_ref[:, h * ns:(h + 1) * ns].T


def _qprep_sample(u_t, cos_t, sin_t):
    ns = u_t.shape[1]
    return pl.pallas_call(
        _qprep_sample_kernel,
        grid=(1,),
        in_specs=[pl.BlockSpec((D_ATT, ns), lambda i: (ROW_Q // D_ATT, 0)),
                  pl.BlockSpec((HEAD_DIM // 2, ns), lambda i: (0, 0)),
                  pl.BlockSpec((HEAD_DIM // 2, ns), lambda i: (0, 0))],
        out_specs=pl.BlockSpec((ATT_HEADS, ns, D_KV), lambda i: (0, 0, 0)),
        out_shape=jax.ShapeDtypeStruct((ATT_HEADS, ns, D_KV), F32),
        scratch_shapes=[pltpu.VMEM((D_KV, ATT_HEADS * ns), F32)],
        compiler_params=_cparams(("arbitrary",)),
        name="qprep_sample",
    )(u_t, cos_t, sin_t)


def _attn_sample_kernel(pt_ref, q_ref, kn_ref, vn_ref, *refs, n_pages, page):
    sb = q_ref.shape[0]
    k_refs = refs[:sb * n_pages]
    v_refs = refs[sb * n_pages:2 * sb * n_pages]
    o_ref, s_ref = refs[2 * sb * n_pages:]
    nb = n_pages * page // MOBA_BLOCK
    nkeys = n_pages * page
    nh = ATT_HEADS

    s_own = []
    for i in range(sb):
        qf = q_ref[i]
        qb = qf.astype(BF16)
        for j in range(n_pages):
            kp = k_refs[i * n_pages + j][...]
            s_ref[i * nh:(i + 1) * nh, j * page:(j + 1) * page] = jnp.dot(
                qb, kp.astype(BF16), preferred_element_type=F32)
        s_own.append(jnp.sum(qf * kn_ref[i], axis=1, keepdims=True))
    s_own = jnp.concatenate(s_own, axis=0)
    lane = lax.broadcasted_iota(jnp.int32, (sb * nh, LANES), 1)
    gate = jnp.zeros((sb * nh, LANES), F32)
    for b in range(nb):
        bsum = jnp.sum(s_ref[:, b * MOBA_BLOCK:(b + 1) * MOBA_BLOCK], axis=1, keepdims=True)
        gate = jnp.where(lane == b, bsum, gate)
    bias = _top3_mask(gate, float(nb), 1)
    expand = (lax.broadcasted_iota(jnp.int32, (LANES, nkeys), 0)
              == (lax.broadcasted_iota(jnp.int32, (LANES, nkeys), 1) >> int(math.log2(MOBA_BLOCK)))).astype(F32)
    sel_keys = jnp.dot(jnp.where(bias == 0.0, 1.0, 0.0), expand, preferred_element_type=F32)
    s = jnp.where(sel_keys > 0.5, s_ref[...], NEG)
    m = jnp.maximum(jnp.max(s, axis=1, keepdims=True), s_own)
    p = jnp.exp(s - m)
    p_own = jnp.exp(s_own - m)
    l = jnp.sum(p, axis=1, keepdims=True) + p_own
    pb = p.astype(BF16)
    nt = (((1,), (1,)), ((), ()))
    hh = lax.broadcasted_iota(jnp.int32, (nh, D_KV), 0) >> int(math.log2(Q_PER_KV))
    cg = lax.broadcasted_iota(jnp.int32, (nh, D_KV), 1) >> int(math.log2(HEAD_DIM))
    for i in range(sb):
        rows = slice(i * nh, (i + 1) * nh)
        acc = p_own[rows] * vn_ref[i]
        for j in range(n_pages):
            acc = acc + lax.dot_general(pb[rows, j * page:(j + 1) * page],
                                        v_refs[i * n_pages + j][...].astype(BF16), nt,
                                        preferred_element_type=F32)
        o_ref[i] = jnp.where(hh == cg, acc / l[rows], 0.0)


ATT_SEQ_BLK = 4


def _attn_sample(page_table_flat, q_rows, k_new, v_new, ck_t, cv_t, n_pages):
    ns = q_rows.shape[0]
    page = ck_t.shape[2]
    sb = ATT_SEQ_BLK

    def pspec(i, j):
        return pl.BlockSpec((None, D_KV, page), lambda n, pt, i=i, j=j: (pt[(n * sb + i) * n_pages + j], 0, 0))

    pages = [pspec(i, j) for i in range(sb) for j in range(n_pages)]
    row3 = lambda n, pt: (n, 0, 0)
    grid_spec = pltpu.PrefetchScalarGridSpec(
        num_scalar_prefetch=1,
        grid=(ns // sb,),
        in_specs=[pl.BlockSpec((sb, ATT_HEADS, D_KV), row3),
                  pl.BlockSpec((sb, 1, D_KV), row3),
                  pl.BlockSpec((sb, 1, D_KV), row3)] + pages + pages,
        out_specs=pl.BlockSpec((sb, ATT_HEADS, D_KV), row3),
        scratch_shapes=[pltpu.VMEM((sb * ATT_HEADS, n_pages * page), F32)],
    )
    return pl.pallas_call(
        functools.partial(_attn_sample_kernel, n_pages=n_pages, page=page),
        grid_spec=grid_spec,
        out_shape=jax.ShapeDtypeStruct((ns, ATT_HEADS, D_KV), F32),
        compiler_params=_cparams(("arbitrary",)),
        name="attn_sample",
    )(page_table_flat, q_rows, k_new, v_new, *([ck_t] * (sb * n_pages)), *([cv_t] * (sb * n_pages)))


def _ocompact_kernel(o_ref, ot_ref):
    for h in range(ATT_HEADS):
        g = h // Q_PER_KV
        ot_ref[h * HEAD_DIM:(h + 1) * HEAD_DIM, :] = o_ref[h].T[g * HEAD_DIM:(g + 1) * HEAD_DIM, :]


def _ocompact(o_h):
    ns = o_h.shape[1]
    return pl.pallas_call(
        _ocompact_kernel,
        grid=(1,),
        in_specs=[pl.BlockSpec((ATT_HEADS, ns, D_KV), lambda i: (0, 0, 0))],
        out_specs=pl.BlockSpec((D_ATT, ns), lambda i: (0, 0)),
        out_shape=jax.ShapeDtypeStruct((D_ATT, ns), F32),
        compiler_params=_cparams(("arbitrary",)),
        name="o_compact",
    )(o_h)


def _rope_tables(pos):
    half = HEAD_DIM // 2
    inv_freq = jnp.power(ROPE_THETA, -jnp.arange(half, dtype=F32) * 2.0 / HEAD_DIM)
    ang = pos.astype(F32)[:, None] * inv_freq[None, :]
    return jnp.cos(ang).T, jnp.sin(ang).T


def _prep_weights(w_in, conv_w, conv_b, dt_bias, a_log, d_skip, ssm_norm_w, w_a_out, w_b_out, w_out):
    sp = np.cumsum([D_SSM, D_SSM, SSM_GROUPS * SSM_STATE, SSM_GROUPS * SSM_STATE, SSM_HEADS,
                    D_ATT, D_KV, D_KV, D_ATT, D_MODEL])
    z_a, x_a, b_a, c_a, dt, q, k, v, z_b, g_a, g_b = jnp.split(w_in, [int(s) for s in sp], axis=1)
    u_cols = jnp.concatenate([q, dt], axis=1)
    u_cols = jnp.pad(u_cols, ((0, 0), (0, UT_ROWS - u_cols.shape[1])))
    w_x = jnp.concatenate([x_a, b_a, c_a], axis=1).astype(BF16)
    g_cols = jnp.concatenate([z_a, z_b, g_a, g_b], axis=1)
    w_t = jnp.concatenate([jnp.concatenate([u_cols[:, j * UT_TN:(j + 1) * UT_TN],
                                            g_cols[:, j * GT_TN:(j + 1) * GT_TN]], axis=1)
                           for j in range(PROJ_STEPS)], axis=1).T.astype(BF16)
    w_kv = jnp.concatenate([k, v], axis=1).T.astype(BF16)
    cw_rows = jnp.pad(jnp.concatenate([conv_w, conv_b[None, :]], axis=0),
                      ((0, CARRY_ROWS - CONV_WIDTH - 1), (0, 0)))
    pad = LANES - SSM_HEADS
    dtb_col = jnp.pad(dt_bias, (0, pad))[:, None]
    alog_col = jnp.pad(a_log, (0, pad))[:, None]
    dsk_col = jnp.repeat(d_skip, SSM_HEAD_DIM)[:, None]
    nw_col = ssm_norm_w[:, None]
    return (w_t, w_x, w_kv, cw_rows, dtb_col, alog_col, dsk_col, nw_col,
            w_a_out.T.astype(BF16), w_b_out.T.astype(BF16), w_out.T.astype(BF16))


def kernel(x_prompt, x_sample, cache_k, cache_v, state_conv, state_ssm, page_table,
           w_in, conv_w, conv_b, dt_bias, a_log, d_skip, ssm_norm_w, w_a_out, w_b_out, w_out, ln_g, ln_b):
    assert w_in.shape[0] == DEPTH
    n_p, l_p, _ = x_prompt.shape
    n_s, l_s, _ = x_sample.shape
    n_pages, page = page_table.shape[1], cache_k.shape[2]
    past_len = n_pages * page
    assert l_s == 1 and l_p % MOBA_BLOCK == 0 and past_len % MOBA_BLOCK == 0 and MOBA_BLOCK % page == 0
    assert n_s % LANES == 0 and past_len // MOBA_BLOCK >= MOBA_TOPK

    (w_t, w_x, w_kv, cw_rows, dtb_col, alog_col, dsk_col, nw_col, wa_t, wb_t, wo_t) = _prep_weights(
        w_in[0], conv_w[0], conv_b[0], dt_bias[0], a_log[0], d_skip[0], ssm_norm_w[0],
        w_a_out[0], w_b_out[0], w_out[0])
    lng, lnb = ln_g[0][None, :], ln_b[0][None, :]

    xp = x_prompt.reshape(n_p * l_p, D_MODEL)
    cos_t, sin_t = _rope_tables(jnp.arange(l_p, dtype=jnp.int32))
    u_t, g_t, xbc_p, kp_t, vp_t, vt3 = _in_proj(xp, w_t, w_x, w_kv, 1024 if l_p % 1024 == 0 else MOBA_BLOCK,
                                                n_p, l_p)
    rope_tile = 4 * MOBA_BLOCK if l_p % (4 * MOBA_BLOCK) == 0 else MOBA_BLOCK
    kp_rt, k_bf, kmean, knmax = _rope_k(kp_t, cos_t, sin_t, rope_tile, MOBA_BLOCK, BF16)
    ya_t, ssm_p = _ssd_prompt(xbc_p, u_t, g_t, n_p, l_p, cw_rows, dtb_col, alog_col, dsk_col, nw_col)
    nb = l_p // MOBA_BLOCK
    kmean_g = jnp.transpose(kmean.reshape(n_p, nb, KV_HEADS, HEAD_DIM), (0, 2, 1, 3))
    o_t = _attn_prompt(u_t, k_bf, vt3, kmean_g, knmax.reshape(n_p, nb, LANES), cos_t, sin_t, n_p, l_p)
    to_cache = lambda a: jnp.transpose(a.reshape(a.shape[0], KV_HEADS, HEAD_DIM, a.shape[2]), (0, 3, 1, 2))[None]
    tm_merge = 2 * MOBA_BLOCK if (n_p * l_p) % (2 * MOBA_BLOCK) == 0 else MOBA_BLOCK
    y_p = _merge(ya_t, o_t, g_t, xp, wa_t, wb_t, wo_t, lng, lnb, tm_merge)
    conv_p = jnp.stack([xbc_p[(b + 1) * l_p - (CONV_WIDTH - 1):(b + 1) * l_p] for b in range(n_p)])

    xs = x_sample.reshape(n_s, D_MODEL)
    pos_s = jnp.full((n_s,), past_len, dtype=jnp.int32)
    cos_s, sin_s = _rope_tables(pos_s)
    us_t, gs_t, xbc_s, ks_t, vs_t, _ = _in_proj(xs, w_t, w_x, w_kv, n_s, 1, n_s)
    k_s = _rope_k(ks_t, cos_s, sin_s, n_s, n_s, F32)[1]
    v_s = vs_t[0].T
    cst = jnp.transpose(state_conv[0], (1, 0, 2))
    st_in = state_ssm[0].reshape(n_s, D_SSM, SSM_STATE)
    ssm_s, yas_t = _ssd_sample(xbc_s, us_t, gs_t, cst, cw_rows, dtb_col, alog_col, dsk_col, nw_col, st_in)
    q_rows = jnp.transpose(_qprep_sample(us_t, cos_s, sin_s), (1, 0, 2))
    pages_t = lambda c: jnp.transpose(c[0], (0, 2, 3, 1)).reshape(c.shape[1], D_KV, page)
    o_s = _attn_sample(page_table.reshape(-1), q_rows, k_s[:, None, :], v_s[:, None, :],
                       pages_t(cache_k), pages_t(cache_v), n_pages)
    os_t = _ocompact(jnp.transpose(o_s, (1, 0, 2)))
    y_s = _merge(yas_t, os_t, gs_t, xs, wa_t, wb_t, wo_t, lng, lnb, n_s)
    conv_s = jnp.concatenate([state_conv[0][:, 1:], xbc_s[:, None, :]], axis=1)

    return (y_p.reshape(n_p, l_p, D_MODEL), y_s.reshape(n_s, l_s, D_MODEL),
            to_cache(kp_rt), to_cache(vp_t),
            k_s.reshape(1, n_s, l_s, KV_HEADS, HEAD_DIM), v_s.reshape(1, n_s, l_s, KV_HEADS, HEAD_DIM),
            conv_p[None], conv_s[None],
            ssm_p.reshape(1, n_p, SSM_HEADS, SSM_HEAD_DIM, SSM_STATE),
            ssm_s.reshape(1, n_s, SSM_HEADS, SSM_HEAD_DIM, SSM_STATE))
```

```python
import functools
import math

import numpy as np
import jax
import jax.numpy as jnp
from jax import lax
from jax.experimental import pallas as pl
from jax.experimental.pallas import tpu as pltpu

F32 = jnp.float32
BF16 = jnp.bfloat16

D_MODEL = 1024
HEAD_DIM = 64
SSM_HEADS = 16
SSM_HEAD_DIM = 64
SSM_GROUPS = 4
SSM_STATE = 128
CONV_WIDTH = 4
D_SSM = SSM_HEADS * SSM_HEAD_DIM
CONV_DIM = D_SSM + 2 * SSM_GROUPS * SSM_STATE
SSD_CHUNK = 128
ATT_HEADS = 16
KV_HEADS = 4
Q_PER_KV = ATT_HEADS // KV_HEADS
D_ATT = ATT_HEADS * HEAD_DIM
D_KV = KV_HEADS * HEAD_DIM
MOBA_BLOCK = 256
MOBA_TOPK = 3
ATT_SCALE = HEAD_DIM ** -0.5
ROPE_THETA = 10000.0
NORM_EPS = 1e-5
DEPTH = 1
DEEPNORM_ALPHA = (2 * DEPTH) ** 0.25

LANES = 128
VMEM_LIMIT = 56 * 1024 * 1024

ROW_Q = 0
ROW_DT = ROW_Q + D_ATT
UT_ROWS = ROW_DT + LANES
ROW_ZA = 0
ROW_ZB = ROW_ZA + D_SSM
ROW_GA = ROW_ZB + D_ATT
ROW_GB = ROW_GA + D_MODEL
GT_ROWS = ROW_GB + D_MODEL
PROJ_STEPS = 4
UT_TN = UT_ROWS // PROJ_STEPS
GT_TN = GT_ROWS // PROJ_STEPS
XT_TN = CONV_DIM // PROJ_STEPS

NEG = -1e30
HI = lax.Precision.HIGHEST


def _sigmoid(x):
    return 0.5 * jnp.tanh(0.5 * x) + 0.5


def _silu(x):
    return x * _sigmoid(x)


def _softplus(x):
    return jnp.maximum(x, 0.0) + jnp.log(1.0 + jnp.exp(-jnp.abs(x)))


def _cparams(sem):
    return pltpu.CompilerParams(dimension_semantics=sem, vmem_limit_bytes=VMEM_LIMIT)


def _inproj_kernel(x_ref, wt_ref, wx_ref, wkv_ref, ut_ref, gt_ref, xt_ref, k_ref, v_ref, vb_ref):
    x = x_ref[...].astype(BF16)
    nt = (((1,), (1,)), ((), ()))
    res = lax.dot_general(wt_ref[...], x, nt, preferred_element_type=F32)
    ut_ref[...] = res[:UT_TN]
    gt_ref[...] = res[UT_TN:].astype(BF16)
    xt_ref[...] = jnp.dot(x, wx_ref[...], preferred_element_type=F32)

    @pl.when(pl.program_id(1) == 0)
    def _():
        kv = lax.dot_general(wkv_ref[...], x, nt, preferred_element_type=F32)
        k_ref[...] = kv[:D_KV]
        v_ref[...] = kv[D_KV:]
        wb = vb_ref.shape[2]
        for c in range(vb_ref.shape[0]):
            vb_ref[c] = kv[D_KV:, c * wb:(c + 1) * wb].astype(BF16)


def _in_proj(x2d, w_t, w_x, w_kv, tm, n, l):
    t = x2d.shape[0]
    per = l // tm
    wb = min(tm, MOBA_BLOCK)
    kv_spec = pl.BlockSpec((None, D_KV, tm), lambda i, j: (i // per, 0, i % per))
    return pl.pallas_call(
        _inproj_kernel,
        grid=(t // tm, PROJ_STEPS),
        in_specs=[pl.BlockSpec((tm, D_MODEL), lambda i, j: (i, 0)),
                  pl.BlockSpec((UT_TN + GT_TN, D_MODEL), lambda i, j: (j, 0)),
                  pl.BlockSpec((D_MODEL, XT_TN), lambda i, j: (0, j)),
                  pl.BlockSpec((2 * D_KV, D_MODEL), lambda i, j: (0, 0))],
        out_specs=[pl.BlockSpec((UT_TN, tm), lambda i, j: (j, i)),
                   pl.BlockSpec((GT_TN, tm), lambda i, j: (j, i)),
                   pl.BlockSpec((tm, XT_TN), lambda i, j: (i, j)), kv_spec, kv_spec,
                   pl.BlockSpec((tm // wb, D_KV, wb), lambda i, j: (i, 0, 0))],
        out_shape=[jax.ShapeDtypeStruct((UT_ROWS, t), F32),
                   jax.ShapeDtypeStruct((GT_ROWS, t), BF16),
                   jax.ShapeDtypeStruct((t, CONV_DIM), F32),
                   jax.ShapeDtypeStruct((n, D_KV, l), F32),
                   jax.ShapeDtypeStruct((n, D_KV, l), F32),
                   jax.ShapeDtypeStruct((t // wb, D_KV, wb), BF16)],
        compiler_params=_cparams(("arbitrary", "arbitrary")),
        name="in_proj",
    )(x2d, w_t, w_x, w_kv)


def _ropek_kernel(k_ref, cos_ref, sin_ref, ko_ref, kt_ref, km_ref, kn_ref):
    half = HEAD_DIM // 2
    nblk = km_ref.shape[0]
    blk = k_ref.shape[1] // nblk
    grp = (lax.broadcasted_iota(jnp.int32, (D_KV, LANES), 0) >> int(math.log2(HEAD_DIM))
           == lax.broadcasted_iota(jnp.int32, (D_KV, LANES), 1)).astype(F32)
    for i in range(nblk):
        sl = slice(i * blk, (i + 1) * blk)
        cos, sin = cos_ref[:, sl], sin_ref[:, sl]
        rows = []
        for g in range(KV_HEADS):
            x1 = k_ref[g * HEAD_DIM:g * HEAD_DIM + half, sl]
            x2 = k_ref[g * HEAD_DIM + half:(g + 1) * HEAD_DIM, sl]
            rows += [x1 * cos - x2 * sin, x2 * cos + x1 * sin]
        kr = jnp.concatenate(rows, axis=0)
        ko_ref[:, sl] = kr
        kt = kr.T
        kq = kt.astype(kt_ref.dtype)
        kt_ref[sl, :] = kq
        km_ref[i] = jnp.mean(kt, axis=0, keepdims=True)
        kf = kq.astype(F32)
        nrm2 = jnp.dot(kf * kf, grp, preferred_element_type=F32)
        kn_ref[i] = jnp.sqrt(jnp.max(nrm2, axis=0, keepdims=True)) * (1.0 + 2.0 ** -7)


def _rope_k(k_t, cos_t, sin_t, tile, blk, tok_dtype):
    n, _, l = k_t.shape
    nt = l // tile
    per = tile // blk
    return pl.pallas_call(
        _ropek_kernel,
        grid=(n, nt),
        in_specs=[pl.BlockSpec((None, D_KV, tile), lambda b, i: (b, 0, i)),
                  pl.BlockSpec((HEAD_DIM // 2, tile), lambda b, i: (0, i)),
                  pl.BlockSpec((HEAD_DIM // 2, tile), lambda b, i: (0, i))],
        out_specs=[pl.BlockSpec((None, D_KV, tile), lambda b, i: (b, 0, i)),
                   pl.BlockSpec((tile, D_KV), lambda b, i: (b * nt + i, 0)),
                   pl.BlockSpec((per, 1, D_KV), lambda b, i: (b * nt + i, 0, 0)),
                   pl.BlockSpec((per, 1, LANES), lambda b, i: (b * nt + i, 0, 0))],
        out_shape=[jax.ShapeDtypeStruct((n, D_KV, l), F32),
                   jax.ShapeDtypeStruct((n * l, D_KV), tok_dtype),
                   jax.ShapeDtypeStruct((n * nt * per, 1, D_KV), F32),
                   jax.ShapeDtypeStruct((n * nt * per, 1, LANES), F32)],
        compiler_params=_cparams(("arbitrary", "arbitrary")),
        name="rope_k",
    )(k_t, cos_t, sin_t)


def _gated_rmsnorm_t(y_t, z_t, w_b):
    g = y_t * _silu(z_t)
    gsz = D_SSM // SSM_GROUPS
    outs = []
    for i in range(SSM_GROUPS):
        gg = g[i * gsz:(i + 1) * gsz]
        ms = jnp.mean(gg * gg, axis=0, keepdims=True)
        outs.append(gg * lax.rsqrt(ms + NORM_EPS))
    return jnp.concatenate(outs, axis=0) * w_b


CARRY_ROWS = 8


def _ssd_prompt_kernel(xbc_ref, za_ref, dt_ref, cw_ref, dtb_ref, alog_ref, dsk_ref, nw_ref,
                       ya_ref, st_ref, stage_ref, pb_ref):
    c = pl.program_id(1)
    L = SSD_CHUNK

    @pl.when((pl.program_id(0) == 0) & (c == 0))
    def _():
        pb_ref[0] = jnp.broadcast_to(dsk_ref[...], pb_ref.shape[1:])
        pb_ref[1] = jnp.broadcast_to(nw_ref[...], pb_ref.shape[1:])

    @pl.when(c == 0)
    def _():
        stage_ref[:CARRY_ROWS, :] = jnp.zeros((CARRY_ROWS, CONV_DIM), F32)
        st_ref[...] = jnp.zeros_like(st_ref)

    for k in range(xbc_ref.shape[0] // L):
        _ssd_chunk(xbc_ref, za_ref, dt_ref, cw_ref, dtb_ref, alog_ref, ya_ref, st_ref, stage_ref, pb_ref,
                   slice(k * L, (k + 1) * L))


def _ssd_chunk(xbc_ref, za_ref, dt_ref, cw_ref, dtb_ref, alog_ref, ya_ref, st_ref, stage_ref, pb_ref, sl):
    L = SSD_CHUNK
    cur = xbc_ref[sl, :]
    stage_ref[CARRY_ROWS:, :] = cur
    acc = cur * cw_ref[CONV_WIDTH - 1:CONV_WIDTH, :] + cw_ref[CONV_WIDTH:CONV_WIDTH + 1, :]
    for s in range(1, CONV_WIDTH):
        acc = acc + stage_ref[CARRY_ROWS - s:CARRY_ROWS - s + L, :] * cw_ref[CONV_WIDTH - 1 - s:CONV_WIDTH - s, :]
    stage_ref[:CARRY_ROWS, :] = cur[L - CARRY_ROWS:, :]
    xbc = _silu(acc)
    nbc = SSM_GROUPS * SSM_STATE
    xs_t = xbc[:, :D_SSM].T
    bm = xbc[:, D_SSM:D_SSM + nbc]
    cm_t = xbc[:, D_SSM + nbc:].T

    dt_t = _softplus(dt_ref[:, sl] + dtb_ref[...])
    a_col = -jnp.exp(alog_ref[...]) * math.log2(math.e)
    adt_t = dt_t * a_col
    ii = lax.broadcasted_iota(jnp.int32, (L, L), 0)
    jj = lax.broadcasted_iota(jnp.int32, (L, L), 1)
    upper = (ii <= jj).astype(F32)
    lower = (jj <= ii).astype(F32)
    acs_t = jnp.dot(adt_t, upper, precision=HI, preferred_element_type=F32)
    acs = lax.dot_general(lower, adt_t, (((1,), (1,)), ((), ())), precision=HI,
                          preferred_element_type=F32)
    causal_t = ii <= jj

    dsk_b = pb_ref[0]
    ys = []
    R = SSM_HEADS // SSM_GROUPS
    P = SSM_HEAD_DIM
    for g in range(SSM_GROUPS):
        cmg_t = cm_t[g * SSM_STATE:(g + 1) * SSM_STATE]
        bmg = bm[:, g * SSM_STATE:(g + 1) * SSM_STATE].astype(BF16)
        cmg_tb = cmg_t.astype(BF16)
        cb_t = jnp.dot(bmg, cmg_tb, preferred_element_type=F32)
        for r in range(R):
            h = g * R + r
            row = acs_t[h:h + 1, :]
            col = acs[:, h:h + 1]
            dec = jnp.exp2(jnp.where(causal_t, row - col, NEG))
            m_t = (cb_t * dec).astype(BF16)
            xs_h = xs_t[h * P:(h + 1) * P]
            xdt_h = xs_h * dt_t[h:h + 1, :]
            y_diag = jnp.dot(xdt_h.astype(BF16), m_t, preferred_element_type=F32)
            hprev = st_ref[0, h * P:(h + 1) * P, :]
            y_off = jnp.dot(hprev.astype(BF16), cmg_tb, preferred_element_type=F32) * jnp.exp2(row)
            last = acs_t[h:h + 1, L - 1:L]
            te = jnp.exp2(last - row)
            st = jnp.dot((xdt_h * te).astype(BF16), bmg, preferred_element_type=F32)
            st_ref[0, h * P:(h + 1) * P, :] = jnp.exp2(last) * hprev + st
            ys.append(y_diag + y_off + dsk_b[h * P:(h + 1) * P] * xs_h)
    y_t = jnp.concatenate(ys, axis=0)
    ya_ref[:, sl] = _gated_rmsnorm_t(y_t, za_ref[:, sl].astype(F32), pb_ref[1]).astype(BF16)


SSD_CHUNKS_PER_STEP = 2


def _ssd_prompt(xbc_tok, u_t, g_t, n, l, cw_rows, dtb_col, alog_col, dsk_col, nw_col):
    tile = SSD_CHUNK * (SSD_CHUNKS_PER_STEP if l % (SSD_CHUNK * SSD_CHUNKS_PER_STEP) == 0 else 1)
    nc = l // tile
    col = lambda b, c: b * nc + c
    const = lambda b, c: (0, 0)
    return pl.pallas_call(
        _ssd_prompt_kernel,
        grid=(n, nc),
        in_specs=[pl.BlockSpec((tile, CONV_DIM), lambda b, c: (col(b, c), 0)),
                  pl.BlockSpec((D_SSM, tile), lambda b, c: (ROW_ZA // D_SSM, col(b, c))),
                  pl.BlockSpec((LANES, tile), lambda b, c: (ROW_DT // LANES, col(b, c))),
                  pl.BlockSpec((CARRY_ROWS, CONV_DIM), const),
                  pl.BlockSpec((LANES, 1), const),
                  pl.BlockSpec((LANES, 1), const),
                  pl.BlockSpec((D_SSM, 1), const),
                  pl.BlockSpec((D_SSM, 1), const)],
        out_specs=[pl.BlockSpec((D_SSM, tile), lambda b, c: (0, col(b, c))),
                   pl.BlockSpec((1, D_SSM, SSM_STATE), lambda b, c: (b, 0, 0))],
        out_shape=[jax.ShapeDtypeStruct((D_SSM, n * l), BF16),
                   jax.ShapeDtypeStruct((n, D_SSM, SSM_STATE), F32)],
        scratch_shapes=[pltpu.VMEM((CARRY_ROWS + SSD_CHUNK, CONV_DIM), F32),
                        pltpu.VMEM((2, D_SSM, LANES), F32)],
        compiler_params=_cparams(("arbitrary", "arbitrary")),
        name="ssd_prompt",
    )(xbc_tok, g_t, u_t, cw_rows, dtb_col, alog_col, dsk_col, nw_col)


def _rope_heads_t(q_ref, cos, sin, dst_ref, scale):
    nq = q_ref.shape[1]
    half = HEAD_DIM // 2
    for h in range(ATT_HEADS):
        x1 = q_ref[h * HEAD_DIM:h * HEAD_DIM + half, :]
        x2 = q_ref[h * HEAD_DIM + half:(h + 1) * HEAD_DIM, :]
        g = h // Q_PER_KV
        dst_ref[g * HEAD_DIM:g * HEAD_DIM + half, h * nq:(h + 1) * nq] = (x1 * cos - x2 * sin) * scale
        dst_ref[g * HEAD_DIM + half:(g + 1) * HEAD_DIM, h * nq:(h + 1) * nq] = (x2 * cos + x1 * sin) * scale


def _top3_mask(gate, nvalid, axis):
    nb = gate.shape[axis]
    bi = lax.broadcasted_iota(jnp.int32, gate.shape, axis).astype(F32)
    valid = bi < nvalid
    gate = jnp.where(valid, gate, -jnp.inf)
    sel = jnp.zeros(gate.shape, F32)
    for _ in range(MOBA_TOPK):
        mx = jnp.max(gate, axis=axis, keepdims=True)
        idx = jnp.min(jnp.where(gate == mx, bi, float(nb)), axis=axis, keepdims=True)
        pick = bi == idx
        sel = jnp.where(pick, 1.0, sel)
        gate = jnp.where(pick, -jnp.inf, gate)
    return jnp.where((sel > 0.0) & valid, 0.0, NEG)


SUM_ROWS = 16
MAX_EXCESS = 60.0
SCORES_AHEAD = 4


def _attn_prompt_kernel(q_ref, cos_ref, sin_ref, k_ref, vt_ref, km_ref, kn_ref, o_ref,
                        qf_ref, qb_ref, bias_ref, m_ref, acc_ref):
    t = pl.program_id(1)
    nq = MOBA_BLOCK

    @pl.when((pl.program_id(0) == 0) & (t == 0))
    def _():
        qf_ref[...] = jnp.zeros_like(qf_ref)

    _rope_heads_t(q_ref, cos_ref[...], sin_ref[...], qf_ref, ATT_SCALE * math.log2(math.e))
    qb_ref[...] = qf_ref[...].astype(BF16)
    ones = jnp.ones((SUM_ROWS, nq), BF16)
    gw = Q_PER_KV * nq
    qn = []
    for g in range(KV_HEADS):
        qg = qf_ref[g * HEAD_DIM:(g + 1) * HEAD_DIM, g * gw:(g + 1) * gw]
        gate = jnp.dot(km_ref[g], qg, precision=HI, preferred_element_type=F32)
        bias_ref[:, g * gw:(g + 1) * gw] = _top3_mask(gate, t.astype(F32), 0)
        qn.append(jnp.sqrt(jnp.sum(qg * qg, axis=0, keepdims=True)) * (1.0 + 2.0 ** -7))
    causal = lax.broadcasted_iota(jnp.int32, (nq, nq), 0) <= lax.broadcasted_iota(jnp.int32, (nq, nq), 1)

    def run_blocks(blocks, fixed_max=False):
        loaded = []
        for bk, own in blocks:
            kb = k_ref[pl.ds(pl.multiple_of(bk * nq, nq), nq), :]
            brow = None if own else bias_ref[pl.ds(bk, 1), :]
            loaded.append((kb, vt_ref[bk], brow, own))
        items = [(j, h) for j in range(len(blocks)) for h in range(ATT_HEADS)]
        qk = lambda j, h: jnp.dot(loaded[j][0], qb_ref[:, h * nq:(h + 1) * nq], preferred_element_type=F32)
        pending = [qk(*it) for it in items[:SCORES_AHEAD]]
        for idx, (j, h) in enumerate(items):
            s = pending.pop(0)
            if idx + SCORES_AHEAD < len(items):
                pending.append(qk(*items[idx + SCORES_AHEAD]))
            _, vtb, brow, own = loaded[j]
            g = h // Q_PER_KV
            cols = slice(h * nq, (h + 1) * nq)
            if own:
                s = jnp.where(causal, s, NEG)
                m_new = jnp.max(s, axis=0, keepdims=True)
                sub = m_new
            elif fixed_max:
                m_new = None
                sub = jnp.where(brow[:, cols] == 0.0, m_ref[:, cols], -NEG)
            else:
                bh = brow[:, cols]
                m_old = m_ref[:, cols]
                m_new = jnp.maximum(m_old, jnp.max(s, axis=0, keepdims=True) + bh)
                sub = jnp.where(bh == 0.0, m_new, -NEG)
            p = jnp.exp2(s - sub).astype(BF16)
            va = jnp.concatenate([vtb[g * HEAD_DIM:(g + 1) * HEAD_DIM, :], ones], axis=0)
            pv = jnp.dot(va, p, preferred_element_type=F32)
            if own:
                acc_ref[h] = pv
                m_ref[:, cols] = m_new
            elif fixed_max:
                acc_ref[h] += pv
            else:
                acc_ref[h] = acc_ref[h] * jnp.exp2(m_old - m_new) + pv
                m_ref[:, cols] = m_new

    run_blocks([(t, True)])

    bi = lax.broadcasted_iota(jnp.int32, kn_ref.shape, 0)
    kmax = jnp.max(jnp.where(bi < t, kn_ref[...], 0.0), axis=0, keepdims=True)
    excess = [jnp.max(qn[g] * kmax[:, g:g + 1] - m_ref[:, g * gw:(g + 1) * gw]) for g in range(KV_HEADS)]
    fixed_ok = functools.reduce(jnp.maximum, excess) <= MAX_EXCESS
    odd = (t & 1) == 1

    @pl.when(fixed_ok)
    def _():
        def body(i, carry):
            run_blocks([(8 * i + j, False) for j in range(8)], fixed_max=True)
            return carry

        lax.fori_loop(0, t >> 3, body, 0)
        done8 = t & ~7
        done = t & ~3

        @pl.when((t & 4) == 4)
        def _():
            run_blocks([(done8 + j, False) for j in range(4)], fixed_max=True)

        @pl.when((t & 2) == 2)
        def _():
            run_blocks([(done, False), (done + 1, False)], fixed_max=True)

        @pl.when(odd)
        def _():
            run_blocks([(t - 1, False)], fixed_max=True)

    @pl.when(jnp.logical_not(fixed_ok))
    def _():
        def body(i, carry):
            run_blocks([(2 * i, False), (2 * i + 1, False)])
            return carry

        lax.fori_loop(0, t >> 1, body, 0)

        @pl.when(odd)
        def _():
            run_blocks([(t - 1, False)])

    for h in range(ATT_HEADS):
        o_ref[h * HEAD_DIM:(h + 1) * HEAD_DIM, :] = (
            acc_ref[h, :HEAD_DIM, :] / acc_ref[h, HEAD_DIM:HEAD_DIM + 1, :]).astype(o_ref.dtype)


def _attn_prompt(u_t, k_bf, vt3, kmean, knmax, cos_t, sin_t, n, l):
    nq = MOBA_BLOCK
    nb = l // nq
    col = lambda b, t: b * nb + t
    return pl.pallas_call(
        _attn_prompt_kernel,
        grid=(n, nb),
        in_specs=[pl.BlockSpec((D_ATT, nq), lambda b, t: (ROW_Q // D_ATT, col(b, t))),
                  pl.BlockSpec((HEAD_DIM // 2, nq), lambda b, t: (0, t)),
                  pl.BlockSpec((HEAD_DIM // 2, nq), lambda b, t: (0, t)),
                  pl.BlockSpec((l, D_KV), lambda b, t: (b, 0)),
                  pl.BlockSpec((nb, D_KV, nq), lambda b, t: (b, 0, 0)),
                  pl.BlockSpec((None, KV_HEADS, nb, HEAD_DIM), lambda b, t: (b, 0, 0, 0)),
                  pl.BlockSpec((None, nb, LANES), lambda b, t: (b, 0, 0))],
        out_specs=pl.BlockSpec((D_ATT, nq), lambda b, t: (0, col(b, t))),
        out_shape=jax.ShapeDtypeStruct((D_ATT, n * l), BF16),
        scratch_shapes=[pltpu.VMEM((D_KV, ATT_HEADS * nq), F32),
                        pltpu.VMEM((D_KV, ATT_HEADS * nq), BF16),
                        pltpu.VMEM((nb, ATT_HEADS * nq), F32),
                        pltpu.VMEM((1, ATT_HEADS * nq), F32),
                        pltpu.VMEM((ATT_HEADS, HEAD_DIM + SUM_ROWS, nq), F32)],
        compiler_params=_cparams(("arbitrary", "arbitrary")),
        name="attn_prompt",
    )(u_t, cos_t, sin_t, k_bf, vt3, kmean, knmax)


def _merge_kernel(ya_ref, o_ref, zb_ref, ga_ref, gb_ref, x_ref, wa_ref, wb_ref, wo_ref, lng_ref, lnb_ref,
                  y_ref):
    br_a = jnp.dot(wa_ref[...], ya_ref[...], preferred_element_type=F32)
    ob = (o_ref[...].astype(F32) * _silu(zb_ref[...].astype(F32))).astype(BF16)
    br_b = jnp.dot(wb_ref[...], ob, preferred_element_type=F32)
    mixed = _sigmoid(ga_ref[...].astype(F32)) * br_a + _sigmoid(gb_ref[...].astype(F32)) * br_b
    out_t = jnp.dot(wo_ref[...], mixed.astype(BF16), preferred_element_type=F32)
    hsum = DEEPNORM_ALPHA * x_ref[...] + out_t.T
    mu = jnp.mean(hsum, axis=-1, keepdims=True)
    d = hsum - mu
    var = jnp.mean(d * d, axis=-1, keepdims=True)
    y_ref[...] = d * lax.rsqrt(var + NORM_EPS) * lng_ref[...] + lnb_ref[...]


def _merge(ya_t, o_t, g_t, x2d, wa_t, wb_t, wo_t, ln_g, ln_b, tm):
    t = x2d.shape[0]
    const = lambda i: (0, 0)
    wspec = pl.BlockSpec((D_MODEL, D_MODEL), const, pipeline_mode=pl.Buffered(1))
    return pl.pallas_call(
        _merge_kernel,
        grid=(t // tm,),
        in_specs=[pl.BlockSpec((D_SSM, tm), lambda i: (0, i)),
                  pl.BlockSpec((D_ATT, tm), lambda i: (0, i)),
                  pl.BlockSpec((D_ATT, tm), lambda i: (ROW_ZB // D_ATT, i)),
                  pl.BlockSpec((D_MODEL, tm), lambda i: (ROW_GA // D_MODEL, i)),
                  pl.BlockSpec((D_MODEL, tm), lambda i: (ROW_GB // D_MODEL, i)),
                  pl.BlockSpec((tm, D_MODEL), lambda i: (i, 0)),
                  wspec, wspec, wspec,
                  pl.BlockSpec((1, D_MODEL), const),
                  pl.BlockSpec((1, D_MODEL), const)],
        out_specs=pl.BlockSpec((tm, D_MODEL), lambda i: (i, 0)),
        out_shape=jax.ShapeDtypeStruct((t, D_MODEL), F32),
        compiler_params=_cparams(("arbitrary",)),
        name="merge",
    )(ya_t, o_t, g_t, g_t, g_t, x2d, wa_t, wb_t, wo_t, ln_g, ln_b)


SEQ_BLK = 8


def _split3(x):
    hi = x.astype(BF16)
    r1 = x - hi.astype(F32)
    mid = r1.astype(BF16)
    lo = (r1 - mid.astype(F32)).astype(BF16)
    return hi, mid, lo


def _ssd_sample_kernel(xbc_ref, za_ref, dt_ref, cst_ref, cw_ref, dtb_ref, alog_ref, dsk_ref, nw_ref, st_ref,
                       sto_ref, ya_ref,
                       dec_ref, xdt_ref, bm_ref, cmt_ref, xs_ref, y_ref):
    i = pl.program_id(0)
    ns = xbc_ref.shape[0]
    P = SSM_HEAD_DIM
    gs = D_SSM // SSM_GROUPS

    @pl.when(i == 0)
    def _():
        pre = xbc_ref[...]
        acc = pre * cw_ref[CONV_WIDTH - 1:CONV_WIDTH, :] + cw_ref[CONV_WIDTH:CONV_WIDTH + 1, :]
        for w in range(CONV_WIDTH - 1):
            acc = acc + cst_ref[w] * cw_ref[w:w + 1, :]
        xbc = _silu(acc)
        nbc = SSM_GROUPS * SSM_STATE
        xs_t = xbc[:, :D_SSM].T
        xs_ref[...] = xs_t
        bm_ref[...] = xbc[:, D_SSM:D_SSM + nbc]
        cmt_ref[...] = xbc[:, D_SSM + nbc:].T
        dt_t = _softplus(dt_ref[...] + dtb_ref[...])
        a_col = -jnp.exp(alog_ref[...])
        dec_t = jnp.exp(dt_t * a_col)
        for h in range(SSM_HEADS):
            rows = slice(h * P, (h + 1) * P)
            dec_ref[rows, :] = jnp.broadcast_to(dec_t[h:h + 1, :], (P, ns))
            xdt_ref[rows, :] = xs_t[rows] * dt_t[h:h + 1, :]
        y_ref[...] = jnp.zeros_like(y_ref)

    kk = lax.broadcasted_iota(jnp.int32, (ns, SEQ_BLK * SSM_STATE), 0)
    cc = lax.broadcasted_iota(jnp.int32, (ns, SEQ_BLK * SSM_STATE), 1)
    sel = (kk == i * SEQ_BLK + (cc >> 7)).astype(BF16)

    def bsel(x):
        hi, mid, lo = _split3(x)
        d = lambda a: jnp.dot(a, sel, preferred_element_type=F32)
        return d(hi) + d(mid) + d(lo)

    dec_b = bsel(dec_ref[...])
    upd = []
    for g in range(SSM_GROUPS):
        bg_all = bm_ref[:, g * SSM_STATE:(g + 1) * SSM_STATE]
        b_sel = jnp.concatenate([bg_all] * SEQ_BLK, axis=1).astype(BF16) * sel
        upd.append(jnp.dot(xdt_ref[g * gs:(g + 1) * gs, :].astype(BF16), b_sel,
                           preferred_element_type=F32))
    lane = lax.broadcasted_iota(jnp.int32, (SSM_STATE, ns), 1)
    for r in range(SEQ_BLK):
        n = i * SEQ_BLK + r
        cs = slice(r * SSM_STATE, (r + 1) * SSM_STATE)
        onehot = lane == n
        for g in range(SSM_GROUPS):
            rows = slice(g * gs, (g + 1) * gs)
            hn = dec_b[rows, cs] * st_ref[r, rows, :] + upd[g][:, cs]
            sto_ref[r, rows, :] = hn
            cg = jnp.where(onehot, cmt_ref[g * SSM_STATE:(g + 1) * SSM_STATE, :], 0.0)
            y_ref[rows, :] += jnp.dot(hn.astype(BF16), cg.astype(BF16), preferred_element_type=F32)

    @pl.when(i == pl.num_programs(0) - 1)
    def _():
        y_t = y_ref[...] + jnp.broadcast_to(dsk_ref[...], y_ref.shape) * xs_ref[...]
        w_b = jnp.broadcast_to(nw_ref[...], y_ref.shape)
        ya_ref[...] = _gated_rmsnorm_t(y_t, za_ref[...].astype(F32), w_b).astype(BF16)


def _ssd_sample(xbc_tok, u_t, g_t, cst, cw_rows, dtb_col, alog_col, dsk_col, nw_col, state):
    ns = u_t.shape[1]
    const2 = lambda i: (0, 0)
    return pl.pallas_call(
        _ssd_sample_kernel,
        grid=(ns // SEQ_BLK,),
        in_specs=[pl.BlockSpec((ns, CONV_DIM), const2),
                  pl.BlockSpec((D_SSM, ns), lambda i: (ROW_ZA // D_SSM, 0)),
                  pl.BlockSpec((LANES, ns), lambda i: (ROW_DT // LANES, 0)),
                  pl.BlockSpec((CONV_WIDTH - 1, ns, CONV_DIM), lambda i: (0, 0, 0)),
                  pl.BlockSpec((CARRY_ROWS, CONV_DIM), const2),
                  pl.BlockSpec((LANES, 1), const2),
                  pl.BlockSpec((LANES, 1), const2),
                  pl.BlockSpec((D_SSM, 1), const2),
                  pl.BlockSpec((D_SSM, 1), const2),
                  pl.BlockSpec((SEQ_BLK, D_SSM, SSM_STATE), lambda i: (i, 0, 0))],
        out_specs=[pl.BlockSpec((SEQ_BLK, D_SSM, SSM_STATE), lambda i: (i, 0, 0)),
                   pl.BlockSpec((D_SSM, ns), const2)],
        out_shape=[jax.ShapeDtypeStruct(state.shape, F32),
                   jax.ShapeDtypeStruct((D_SSM, ns), BF16)],
        scratch_shapes=[pltpu.VMEM((D_SSM, ns), F32),
                        pltpu.VMEM((D_SSM, ns), F32),
                        pltpu.VMEM((ns, SSM_GROUPS * SSM_STATE), F32),
                        pltpu.VMEM((SSM_GROUPS * SSM_STATE, ns), F32),
                        pltpu.VMEM((D_SSM, ns), F32),
                        pltpu.VMEM((D_SSM, ns), F32)],
        compiler_params=_cparams(("arbitrary",)),
        name="ssd_sample",
    )(xbc_tok, g_t, u_t, cst, cw_rows, dtb_col, alog_col, dsk_col, nw_col, state)


def _qprep_sample_kernel(q_ref, cos_ref, sin_ref, qt_ref, qf_ref):
    ns = q_ref.shape[1]
    qf_ref[...] = jnp.zeros_like(qf_ref)
    _rope_heads_t(q_ref, cos_ref[...], sin_ref[...], qf_ref, ATT_SCALE)
    for h in range(ATT_HEADS):
        qt_ref[h] = qf_ref[:, h * ns:(h + 1) * ns].T


def _qprep_sample(u_t, cos_t, sin_t):
    ns = u_t.shape[1]
    return pl.pallas_call(
        _qprep_sample_kernel,
        grid=(1,),
        in_specs=[pl.BlockSpec((D_ATT, ns), lambda i: (ROW_Q // D_ATT, 0)),
                  pl.BlockSpec((HEAD_DIM // 2, ns), lambda i: (0, 0)),
                  pl.BlockSpec((HEAD_DIM // 2, ns), lambda i: (0, 0))],
        out_specs=pl.BlockSpec((ATT_HEADS, ns, D_KV), lambda i: (0, 0, 0)),
        out_shape=jax.ShapeDtypeStruct((ATT_HEADS, ns, D_KV), F32),
        scratch_shapes=[pltpu.VMEM((D_KV, ATT_HEADS * ns), F32)],
        compiler_params=_cparams(("arbitrary",)),
        name="qprep_sample",
    )(u_t, cos_t, sin_t)


def _attn_sample_kernel(pt_ref, q_ref, kn_ref, vn_ref, *refs, n_pages, page):
    sb = q_ref.shape[0]
    k_refs = refs[:sb * n_pages]
    v_refs = refs[sb * n_pages:2 * sb * n_pages]
    o_ref, s_ref = refs[2 * sb * n_pages:]
    nb = n_pages * page // MOBA_BLOCK
    nkeys = n_pages * page
    nh = ATT_HEADS

    s_own = []
    for i in range(sb):
        qf = q_ref[i]
        qb = qf.astype(BF16)
        for j in range(n_pages):
            kp = k_refs[i * n_pages + j][...]
            s_ref[i * nh:(i + 1) * nh, j * page:(j + 1) * page] = jnp.dot(
                qb, kp.astype(BF16), preferred_element_type=F32)
        s_own.append(jnp.sum(qf * kn_ref[i], axis=1, keepdims=True))
    s_own = jnp.concatenate(s_own, axis=0)
    lane = lax.broadcasted_iota(jnp.int32, (sb * nh, LANES), 1)
    gate = jnp.zeros((sb * nh, LANES), F32)
    for b in range(nb):
        bsum = jnp.sum(s_ref[:, b * MOBA_BLOCK:(b + 1) * MOBA_BLOCK], axis=1, keepdims=True)
        gate = jnp.where(lane == b, bsum, gate)
    bias = _top3_mask(gate, float(nb), 1)
    expand = (lax.broadcasted_iota(jnp.int32, (LANES, nkeys), 0)
              == (lax.broadcasted_iota(jnp.int32, (LANES, nkeys), 1) >> int(math.log2(MOBA_BLOCK)))).astype(F32)
    sel_keys = jnp.dot(jnp.where(bias == 0.0, 1.0, 0.0), expand, preferred_element_type=F32)
    s = jnp.where(sel_keys > 0.5, s_ref[...], NEG)
    m = jnp.maximum(jnp.max(s, axis=1, keepdims=True), s_own)
    p = jnp.exp(s - m)
    p_own = jnp.exp(s_own - m)
    l = jnp.sum(p, axis=1, keepdims=True) + p_own
    pb = p.astype(BF16)
    nt = (((1,), (1,)), ((), ()))
    hh = lax.broadcasted_iota(jnp.int32, (nh, D_KV), 0) >> int(math.log2(Q_PER_KV))
    cg = lax.broadcasted_iota(jnp.int32, (nh, D_KV), 1) >> int(math.log2(HEAD_DIM))
    for i in range(sb):
        rows = slice(i * nh, (i + 1) * nh)
        acc = p_own[rows] * vn_ref[i]
        for j in range(n_pages):
            acc = acc + lax.dot_general(pb[rows, j * page:(j + 1) * page],
                                        v_refs[i * n_pages + j][...].astype(BF16), nt,
                                        preferred_element_type=F32)
        o_ref[i] = jnp.where(hh == cg, acc / l[rows], 0.0)


ATT_SEQ_BLK = 4


def _attn_sample(page_table_flat, q_rows, k_new, v_new, ck_t, cv_t, n_pages):
    ns = q_rows.shape[0]
    page = ck_t.shape[2]
    sb = ATT_SEQ_BLK

    def pspec(i, j):
        return pl.BlockSpec((None, D_KV, page), lambda n, pt, i=i, j=j: (pt[(n * sb + i) * n_pages + j], 0, 0))

    pages = [pspec(i, j) for i in range(sb) for j in range(n_pages)]
    row3 = lambda n, pt: (n, 0, 0)
    grid_spec = pltpu.PrefetchScalarGridSpec(
        num_scalar_prefetch=1,
        grid=(ns // sb,),
        in_specs=[pl.BlockSpec((sb, ATT_HEADS, D_KV), row3),
                  pl.BlockSpec((sb, 1, D_KV), row3),
                  pl.BlockSpec((sb, 1, D_KV), row3)] + pages + pages,
        out_specs=pl.BlockSpec((sb, ATT_HEADS, D_KV), row3),
        scratch_shapes=[pltpu.VMEM((sb * ATT_HEADS, n_pages * page), F32)],
    )
    return pl.pallas_call(
        functools.partial(_attn_sample_kernel, n_pages=n_pages, page=page),
        grid_spec=grid_spec,
        out_shape=jax.ShapeDtypeStruct((ns, ATT_HEADS, D_KV), F32),
        compiler_params=_cparams(("arbitrary",)),
        name="attn_sample",
    )(page_table_flat, q_rows, k_new, v_new, *([ck_t] * (sb * n_pages)), *([cv_t] * (sb * n_pages)))


def _ocompact_kernel(o_ref, ot_ref):
    for h in range(ATT_HEADS):
        g = h // Q_PER_KV
        ot_ref[h * HEAD_DIM:(h + 1) * HEAD_DIM, :] = o_ref[h].T[g * HEAD_DIM:(g + 1) * HEAD_DIM, :]


def _ocompact(o_h):
    ns = o_h.shape[1]
    return pl.pallas_call(
        _ocompact_kernel,
        grid=(1,),
        in_specs=[pl.BlockSpec((ATT_HEADS, ns, D_KV), lambda i: (0, 0, 0))],
        out_specs=pl.BlockSpec((D_ATT, ns), lambda i: (0, 0)),
        out_shape=jax.ShapeDtypeStruct((D_ATT, ns), F32),
        compiler_params=_cparams(("arbitrary",)),
        name="o_compact",
    )(o_h)


def _rope_tables(pos):
    half = HEAD_DIM // 2
    inv_freq = jnp.power(ROPE_THETA, -jnp.arange(half, dtype=F32) * 2.0 / HEAD_DIM)
    ang = pos.astype(F32)[:, None] * inv_freq[None, :]
    return jnp.cos(ang).T, jnp.sin(ang).T


def _prep_weights(w_in, conv_w, conv_b, dt_bias, a_log, d_skip, ssm_norm_w, w_a_out, w_b_out, w_out):
    sp = np.cumsum([D_SSM, D_SSM, SSM_GROUPS * SSM_STATE, SSM_GROUPS * SSM_STATE, SSM_HEADS,
                    D_ATT, D_KV, D_KV, D_ATT, D_MODEL])
    z_a, x_a, b_a, c_a, dt, q, k, v, z_b, g_a, g_b = jnp.split(w_in, [int(s) for s in sp], axis=1)
    u_cols = jnp.concatenate([q, dt], axis=1)
    u_cols = jnp.pad(u_cols, ((0, 0), (0, UT_ROWS - u_cols.shape[1])))
    w_x = jnp.concatenate([x_a, b_a, c_a], axis=1).astype(BF16)
    g_cols = jnp.concatenate([z_a, z_b, g_a, g_b], axis=1)
    w_t = jnp.concatenate([jnp.concatenate([u_cols[:, j * UT_TN:(j + 1) * UT_TN],
                                            g_cols[:, j * GT_TN:(j + 1) * GT_TN]], axis=1)
                           for j in range(PROJ_STEPS)], axis=1).T.astype(BF16)
    w_kv = jnp.concatenate([k, v], axis=1).T.astype(BF16)
    cw_rows = jnp.pad(jnp.concatenate([conv_w, conv_b[None, :]], axis=0),
                      ((0, CARRY_ROWS - CONV_WIDTH - 1), (0, 0)))
    pad = LANES - SSM_HEADS
    dtb_col = jnp.pad(dt_bias, (0, pad))[:, None]
    alog_col = jnp.pad(a_log, (0, pad))[:, None]
    dsk_col = jnp.repeat(d_skip, SSM_HEAD_DIM)[:, None]
    nw_col = ssm_norm_w[:, None]
    return (w_t, w_x, w_kv, cw_rows, dtb_col, alog_col, dsk_col, nw_col,
            w_a_out.T.astype(BF16), w_b_out.T.astype(BF16), w_out.T.astype(BF16))


def kernel(x_prompt, x_sample, cache_k, cache_v, state_conv, state_ssm, page_table,
           w_in, conv_w, conv_b, dt_bias, a_log, d_skip, ssm_norm_w, w_a_out, w_b_out, w_out, ln_g, ln_b):
    assert w_in.shape[0] == DEPTH
    n_p, l_p, _ = x_prompt.shape
    n_s, l_s, _ = x_sample.shape
    n_pages, page = page_table.shape[1], cache_k.shape[2]
    past_len = n_pages * page
    assert l_s == 1 and l_p % MOBA_BLOCK == 0 and past_len % MOBA_BLOCK == 0 and MOBA_BLOCK % page == 0
    assert n_s % LANES == 0 and past_len // MOBA_BLOCK >= MOBA_TOPK

    (w_t, w_x, w_kv, cw_rows, dtb_col, alog_col, dsk_col, nw_col, wa_t, wb_t, wo_t) = _prep_weights(
        w_in[0], conv_w[0], conv_b[0], dt_bias[0], a_log[0], d_skip[0], ssm_norm_w[0],
        w_a_out[0], w_b_out[0], w_out[0])
    lng, lnb = ln_g[0][None, :], ln_b[0][None, :]

    xp = x_prompt.reshape(n_p * l_p, D_MODEL)
    cos_t, sin_t = _rope_tables(jnp.arange(l_p, dtype=jnp.int32))
    u_t, g_t, xbc_p, kp_t, vp_t, vt3 = _in_proj(xp, w_t, w_x, w_kv, 1024 if l_p % 1024 == 0 else MOBA_BLOCK,
                                                n_p, l_p)
    rope_tile = 4 * MOBA_BLOCK if l_p % (4 * MOBA_BLOCK) == 0 else MOBA_BLOCK
    kp_rt, k_bf, kmean, knmax = _rope_k(kp_t, cos_t, sin_t, rope_tile, MOBA_BLOCK, BF16)
    ya_t, ssm_p = _ssd_prompt(xbc_p, u_t, g_t, n_p, l_p, cw_rows, dtb_col, alog_col, dsk_col, nw_col)
    nb = l_p // MOBA_BLOCK
    kmean_g = jnp.transpose(kmean.reshape(n_p, nb, KV_HEADS, HEAD_DIM), (0, 2, 1, 3))
    o_t = _attn_prompt(u_t, k_bf, vt3, kmean_g, knmax.reshape(n_p, nb, LANES), cos_t, sin_t, n_p, l_p)
    to_cache = lambda a: jnp.transpose(a.reshape(a.shape[0], KV_HEADS, HEAD_DIM, a.shape[2]), (0, 3, 1, 2))[None]
    tm_merge = 2 * MOBA_BLOCK if (n_p * l_p) % (2 * MOBA_BLOCK) == 0 else MOBA_BLOCK
    y_p = _merge(ya_t, o_t, g_t, xp, wa_t, wb_t, wo_t, lng, lnb, tm_merge)
    conv_p = jnp.stack([xbc_p[(b + 1) * l_p - (CONV_WIDTH - 1):(b + 1) * l_p] for b in range(n_p)])

    xs = x_sample.reshape(n_s, D_MODEL)
    pos_s = jnp.full((n_s,), past_len, dtype=jnp.int32)
    cos_s, sin_s = _rope_tables(pos_s)
    us_t, gs_t, xbc_s, ks_t, vs_t, _ = _in_proj(xs, w_t, w_x, w_kv, n_s, 1, n_s)
    k_s = _rope_k(ks_t, cos_s, sin_s, n_s, n_s, F32)[1]
    v_s = vs_t[0].T
    cst = jnp.transpose(state_conv[0], (1, 0, 2))
    st_in = state_ssm[0].reshape(n_s, D_SSM, SSM_STATE)
    ssm_s, yas_t = _ssd_sample(xbc_s, us_t, gs_t, cst, cw_rows, dtb_col, alog_col, dsk_col, nw_col, st_in)
    q_rows = jnp.transpose(_qprep_sample(us_t, cos_s, sin_s), (1, 0, 2))
    pages_t = lambda c: jnp.transpose(c[0], (0, 2, 3, 1)).reshape(c.shape[1], D_KV, page)
    o_s = _attn_sample(page_table.reshape(-1), q_rows, k_s[:, None, :], v_s[:, None, :],
                       pages_t(cache_k), pages_t(cache_v), n_pages)
    os_t = _ocompact(jnp.transpose(o_s, (1, 0, 2)))
    y_s = _merge(yas_t, os_t, gs_t, xs, wa_t, wb_t, wo_t, lng, lnb, n_s)
    conv_s = jnp.concatenate([state_conv[0][:, 1:], xbc_s[:, None, :]], axis=1)

    return (y_p.reshape(n_p, l_p, D_MODEL), y_s.reshape(n_s, l_s, D_MODEL),
            to_cache(kp_rt), to_cache(vp_t),
            k_s.reshape(1, n_s, l_s, KV_HEADS, HEAD_DIM), v_s.reshape(1, n_s, l_s, KV_HEADS, HEAD_DIM),
            conv_p[None], conv_s[None],
            ssm_p.reshape(1, n_p, SSM_HEADS, SSM_HEAD_DIM, SSM_STATE),
            ssm_s.reshape(1, n_s, SSM_HEADS, SSM_HEAD_DIM, SSM_STATE))
```
